```python
import math
import jax, jax.numpy as jnp
from jax import lax
import numpy as np

D_MODEL = 1024
BATCH = 1
SEQ = 16384
DEPTH = 2

GRID_W = 64
CTX_LEN = 256

MLA_HEADS = 8
MLA_NOPE = 64
MLA_ROPE = 32
MLA_V = 64
MLA_QK = MLA_NOPE + MLA_ROPE
MLA_Q_LORA = 384
MLA_KV_LORA = 256

NA_HEADS = 8
NA_HEAD_DIM = 64
NA_WIDTH = NA_HEADS * NA_HEAD_DIM
NA_WIN_ROWS = 8
NA_WIN_COLS = 16

RW_HEADS = 8
RW_HEAD_DIM = 64
RW_WIDTH = RW_HEADS * RW_HEAD_DIM
RW_DECAY_LORA = 64
RW_ICLR_LORA = 64
RW_GATE_LORA = 128
RW_GN_EPS = 64e-5
RW_IN = 3 * RW_WIDTH + 2 * RW_DECAY_LORA + 2 * RW_ICLR_LORA + RW_GATE_LORA

N_BRANCHES = 3
IN_SIZES = (MLA_Q_LORA, MLA_KV_LORA, MLA_ROPE, NA_WIDTH, NA_WIDTH, NA_WIDTH, RW_IN, N_BRANCHES * D_MODEL)
IN_WIDTH = sum(IN_SIZES)

FFN_DENSE = 2816
N_EXPERTS = 8
TOP_K = 2
FFN_EXPERT = 3584
MOE_BLOCK = 128
N_DENSE = (DEPTH + 1) // 2
N_MOE = DEPTH // 2

Q_BLOCK = 128
ROPE_BASE = 10000.0
NORM_EPS = 1e-6

kernel_name = "hybrid_mla_natten_rwkv7_moe_dit_block"


def rms_norm(x, g, eps=NORM_EPS):
    xf = x.astype(jnp.float32)
    y = xf * lax.rsqrt(jnp.mean(xf * xf, axis=-1, keepdims=True) + eps)
    return (y * g.astype(jnp.float32)).astype(x.dtype)


def split_last(t, sizes):
    return jnp.split(t, np.cumsum(sizes)[:-1].tolist(), axis=-1)


def to_heads(t, n_heads):
    return t.reshape(t.shape[:-1] + (n_heads, t.shape[-1] // n_heads))


def rope_axis(t, pos):
    half = t.shape[-1] // 2
    freqs = jnp.exp(-math.log(ROPE_BASE) * jnp.arange(half, dtype=jnp.float32) / half)
    ang = pos.astype(jnp.float32)[:, None] * freqs[None, :]
    cos, sin = jnp.cos(ang)[:, None, :], jnp.sin(ang)[:, None, :]
    t1 = t[..., :half].astype(jnp.float32)
    t2 = t[..., half:].astype(jnp.float32)
    return jnp.concatenate([t1 * cos - t2 * sin, t1 * sin + t2 * cos], axis=-1).astype(t.dtype)


def rope_2d_tail(t, n_pass, rows, cols):
    tail = t[..., n_pass:]
    a = tail.shape[-1] // 2
    return jnp.concatenate([t[..., :n_pass], rope_axis(tail[..., :a], rows), rope_axis(tail[..., a:], cols)], axis=-1)


def block_attention(q, k, v):
    B, Lq, H, dq = q.shape
    scale = dq ** -0.5
    nb = Lq // Q_BLOCK
    qb = q.reshape(B, nb, Q_BLOCK, H, dq).swapaxes(0, 1)

    def one(q_blk):
        s = jnp.einsum('bqhd,bkhd->bhqk', q_blk, k, preferred_element_type=jnp.float32) * scale
        p = jax.nn.softmax(s, axis=-1).astype(v.dtype)
        return jnp.einsum('bhqk,bkhd->bqhd', p, v)

    o = lax.map(one, qb)
    return o.swapaxes(0, 1).reshape(B, Lq, H, v.shape[-1])


def mla_queries(cq, cq_g, wuq, qn_g):
    q = to_heads(rms_norm(cq, cq_g) @ wuq, MLA_HEADS)
    return rms_norm(q, qn_g)


def mla_keys_values(ckv, kr, ckv_g, wukv, kn_g):
    kv = to_heads(rms_norm(ckv, ckv_g) @ wukv, MLA_HEADS)
    k_nope, v = kv[..., :MLA_NOPE], kv[..., MLA_NOPE:]
    k_rope = jnp.broadcast_to(kr[:, :, None, :], k_nope.shape[:-1] + (MLA_ROPE,))
    k = rms_norm(jnp.concatenate([k_nope, k_rope], axis=-1), kn_g)
    return k, v


def neighbourhood_attention(q, k, v, k_ctx, v_ctx, rpb):
    B, L, H, dh = q.shape
    n_rows = L // GRID_W
    kr = min(NA_WIN_ROWS, n_rows)
    kw = NA_WIN_COLS
    scale = dh ** -0.5
    kg = k.reshape(B, n_rows, GRID_W, H, dh)
    vg = v.reshape(B, n_rows, GRID_W, H, dh)
    qg = q.reshape(B, n_rows, GRID_W, H, dh).swapaxes(0, 1)
    cols = jnp.arange(GRID_W)
    col_start = jnp.clip(cols - kw // 2, 0, GRID_W - kw)
    col_idx = col_start[:, None] + jnp.arange(kw)[None, :]
    dc = col_idx - cols[:, None] + (NA_WIN_COLS - 1)

    def one_row(args):
        q_row, r = args
        rs = jnp.clip(r - kr // 2, 0, n_rows - kr)
        k_nb = lax.dynamic_slice_in_dim(kg, rs, kr, axis=1)[:, :, col_idx]
        v_nb = lax.dynamic_slice_in_dim(vg, rs, kr, axis=1)[:, :, col_idx]
        dr = rs + jnp.arange(kr) - r + (NA_WIN_ROWS - 1)
        bias = rpb[:, dr[None, :, None], dc[:, None, :]].astype(jnp.float32)
        s_loc = jnp.einsum('bqhd,brqjhd->bhqrj', q_row, k_nb, preferred_element_type=jnp.float32) * scale + bias[None]
        s_ctx = jnp.einsum('bqhd,bchd->bhqc', q_row, k_ctx, preferred_element_type=jnp.float32) * scale
        s = jnp.concatenate([s_loc.reshape(B, H, GRID_W, kr * kw), s_ctx], axis=-1)
        p = jax.nn.softmax(s, axis=-1).astype(v.dtype)
        p_loc = p[..., :kr * kw].reshape(B, H, GRID_W, kr, kw)
        p_ctx = p[..., kr * kw:]
        return jnp.einsum('bhqrj,brqjhd->bqhd', p_loc, v_nb) + jnp.einsum('bhqc,bchd->bqhd', p_ctx, v_ctx)

    o = lax.map(one_row, (qg, jnp.arange(n_rows)))
    return o.swapaxes(0, 1).reshape(B, L, H * dh)


def token_shift_centred(z, mu):
    prev = jnp.pad(z, ((0, 0), (1, 0), (0, 0)))[:, :-1]
    nxt = jnp.pad(z, ((0, 0), (0, 1), (0, 0)))[:, 1:]
    return z + mu[0] * (prev - z) + mu[1] * (nxt - z)


def rwkv_prepare(z, mu, w0, w2, a0, a2, k_k, k_a):
    B, L, _ = z.shape
    z = token_shift_centred(z.astype(jnp.float32), mu)
    r, k, v, wd, ad, gd = split_last(z, (RW_WIDTH, RW_WIDTH, RW_WIDTH, 2 * RW_DECAY_LORA, 2 * RW_ICLR_LORA, RW_GATE_LORA))
    wd = wd.reshape(B, L, 2, RW_DECAY_LORA)
    ad = ad.reshape(B, L, 2, RW_ICLR_LORA)
    w_log = -jax.nn.softplus(-(w0 + jnp.einsum('bldr,drc->bldc', jnp.tanh(wd), w2))) - 0.5
    decay = jnp.exp(-jnp.exp(w_log))
    a = jax.nn.sigmoid(a0 + jnp.einsum('bldr,drc->bldc', ad, a2))
    kk = to_heads(k * k_k, RW_HEADS)
    kk = kk / jnp.maximum(jnp.sqrt(jnp.sum(kk * kk, axis=-1, keepdims=True)), 1e-12)
    k_dir = k[:, :, None, :] * (1.0 + (a - 1.0) * k_a)
    return (to_heads(r, RW_HEADS), to_heads(v, RW_HEADS), kk, to_heads(decay, RW_HEADS),
            to_heads(a, RW_HEADS), to_heads(k_dir, RW_HEADS), gd)


def rwkv_scan(r, w, k, v, kk, a, s0, reverse):
    def step(S, inp):
        r_t, w_t, k_t, v_t, kk_t, a_t = inp
        sa = jnp.einsum('bhij,bhj->bhi', S, -kk_t)
        S = S * w_t[:, :, None, :] + sa[..., None] * (kk_t * a_t)[:, :, None, :] + v_t[..., None] * k_t[:, :, None, :]
        return S, jnp.einsum('bhij,bhj->bhi', S, r_t)

    xs = tuple(t.swapaxes(0, 1) for t in (r, w, k, v, kk, a))
    S, ys = lax.scan(step, s0, xs, reverse=reverse)
    return S, ys.swapaxes(0, 1)


def rwkv_output(r, v, k_dir, gd, y, g2, r_k, ln_g, ln_b, out_dtype):
    B, L = y.shape[:2]
    mean = jnp.mean(y, axis=-1, keepdims=True)
    var = jnp.mean(jnp.square(y - mean), axis=-1, keepdims=True)
    yn = ((y - mean) * lax.rsqrt(var + RW_GN_EPS)).reshape(B, L, RW_WIDTH) * ln_g + ln_b
    bonus = (jnp.sum(r[:, :, None] * k_dir * r_k, axis=-1, keepdims=True) * v[:, :, None]).sum(axis=2)
    g = jax.nn.sigmoid(gd) @ g2
    return ((yn + bonus.reshape(B, L, RW_WIDTH)) * g).astype(out_dtype)


def merge_branches(ya, yb, yr, gates, wo_a, wo_b, wo_r, w_o):
    g_a, g_b, g_r = jnp.split(jax.nn.sigmoid(gates), N_BRANCHES, axis=-1)
    return (g_a * (ya @ wo_a) + g_b * (yb @ wo_b) + g_r * (yr @ wo_r)) @ w_o


def swiglu(h, w1, w3, w2):
    return (jax.nn.silu(h @ w1) * (h @ w3)) @ w2


def moe_swiglu(h, router_w, w1, w3, w2):
    n_tok, d = h.shape
    logits = jnp.matmul(h, router_w, preferred_element_type=jnp.float32)
    top_logit, top_idx = lax.top_k(logits, TOP_K)
    gate = jax.nn.softmax(top_logit, axis=-1)
    n_assign = n_tok * TOP_K
    expert = top_idx.reshape(-1).astype(jnp.int32)
    token = jnp.repeat(jnp.arange(n_tok, dtype=jnp.int32), TOP_K)
    weight = gate.reshape(-1)
    order = jnp.argsort(expert)
    expert_s, token_s, weight_s = expert[order], token[order], weight[order]
    counts = jnp.bincount(expert, length=N_EXPERTS).astype(jnp.int32)
    starts = jnp.cumsum(counts) - counts
    padded = (counts + MOE_BLOCK - 1) // MOE_BLOCK * MOE_BLOCK
    pad_ends = jnp.cumsum(padded)
    pad_starts = pad_ends - padded
    dest = pad_starts[expert_s] + jnp.arange(n_assign, dtype=jnp.int32) - starts[expert_s]
    n_blocks = -(-n_assign // MOE_BLOCK) + N_EXPERTS
    buf_tok = jnp.zeros((n_blocks * MOE_BLOCK,), jnp.int32).at[dest].set(token_s)
    buf_w = jnp.zeros((n_blocks * MOE_BLOCK,), jnp.float32).at[dest].set(weight_s)
    blk_start = jnp.arange(n_blocks, dtype=jnp.int32) * MOE_BLOCK
    blk_expert = jnp.minimum(jnp.searchsorted(pad_ends, blk_start, side='right'), N_EXPERTS - 1)

    def one_block(args):
        idx, wt, e = args
        xb = h[idx]
        hid = jax.nn.silu(xb @ w1[e]) * (xb @ w3[e])
        return (hid @ w2[e]) * wt[:, None].astype(h.dtype)

    y = lax.map(one_block, (buf_tok.reshape(n_blocks, MOE_BLOCK), buf_w.reshape(n_blocks, MOE_BLOCK), blk_expert))
    return jnp.zeros_like(h).at[buf_tok].add(y.reshape(-1, d))


def setup_inputs(seed: int = 0) -> dict:
    key = jax.random.key(seed)
    ks = iter(jax.random.split(key, 64))
    D = D_MODEL

    def nrm(shape, s):
        return jax.random.normal(next(ks), shape, jnp.float32) * s

    def gain(shape):
        return 1.0 + nrm(shape, 0.05)

    return {
        "x": nrm((BATCH, SEQ, D), 1.0),
        "c": nrm((BATCH, D), 1.0),
        "ctx": nrm((BATCH, CTX_LEN, D), 1.0),
        "c_ctx": nrm((D,), 1.0),
        "mod_w": nrm((DEPTH, D, 6 * D), 0.5 * D ** -0.5),
        "mod_b": nrm((DEPTH, 6 * D), 0.02),
        "norm1_g": gain((DEPTH, D)),
        "norm2_g": gain((DEPTH, D)),
        "w_in": nrm((DEPTH, D, IN_WIDTH), D ** -0.5),
        "mla_cq_g": gain((DEPTH, MLA_Q_LORA)),
        "mla_wuq": nrm((DEPTH, MLA_Q_LORA, MLA_HEADS * MLA_QK), MLA_Q_LORA ** -0.5),
        "mla_ckv_g": gain((DEPTH, MLA_KV_LORA)),
        "mla_wukv": nrm((DEPTH, MLA_KV_LORA, MLA_HEADS * (MLA_NOPE + MLA_V)), MLA_KV_LORA ** -0.5),
        "mla_qn_g": gain((DEPTH, MLA_QK)),
        "mla_kn_g": gain((DEPTH, MLA_QK)),
        "mla_wo": nrm((DEPTH, MLA_HEADS * MLA_V, D), (MLA_HEADS * MLA_V) ** -0.5),
        "na_qn_g": gain((DEPTH, NA_HEAD_DIM)),
        "na_kn_g": gain((DEPTH, NA_HEAD_DIM)),
        "na_rpb": nrm((DEPTH, NA_HEADS, 2 * NA_WIN_ROWS - 1, 2 * NA_WIN_COLS - 1), 0.1),
        "na_wo": nrm((DEPTH, NA_WIDTH, D), NA_WIDTH ** -0.5),
        "rw_mu": jax.random.uniform(next(ks), (DEPTH, 2, RW_IN), jnp.float32, 0.0, 0.5),
        "rw_w0": jax.random.uniform(next(ks), (DEPTH, 2, RW_WIDTH), jnp.float32, -6.0, 1.0),
        "rw_w2": nrm((DEPTH, 2, RW_DECAY_LORA, RW_WIDTH), RW_DECAY_LORA ** -0.5),
        "rw_a0": nrm((DEPTH, 2, RW_WIDTH), 0.5),
        "rw_a2": nrm((DEPTH, 2, RW_ICLR_LORA, RW_WIDTH), RW_ICLR_LORA ** -0.5),
        "rw_g2": nrm((DEPTH, RW_GATE_LORA, RW_WIDTH), RW_GATE_LORA ** -0.5),
        "rw_kk": 0.85 + nrm((DEPTH, RW_WIDTH), 0.05),
        "rw_ka": gain((DEPTH, RW_WIDTH)),
        "rw_rk": nrm((DEPTH, RW_HEADS, RW_HEAD_DIM), 0.1),
        "rw_ln_g": gain((DEPTH, RW_WIDTH)),
        "rw_ln_b": nrm((DEPTH, RW_WIDTH), 0.02),
        "rw_wo": nrm((DEPTH, RW_WIDTH, D), RW_WIDTH ** -0.5),
        "w_out": nrm((DEPTH, D, D), D ** -0.5),
        "ffn_w1": nrm((N_DENSE, D, FFN_DENSE), D ** -0.5),
        "ffn_w3": nrm((N_DENSE, D, FFN_DENSE), D ** -0.5),
        "ffn_w2": nrm((N_DENSE, FFN_DENSE, D), FFN_DENSE ** -0.5),
        "moe_router": nrm((N_MOE, D, N_EXPERTS), D ** -0.5),
        "moe_w1": nrm((N_MOE, N_EXPERTS, D, FFN_EXPERT), D ** -0.5),
        "moe_w3": nrm((N_MOE, N_EXPERTS, D, FFN_EXPERT), D ** -0.5),
        "moe_w2": nrm((N_MOE, N_EXPERTS, FFN_EXPERT, D), FFN_EXPERT ** -0.5),
    }


def reference(x, c, ctx, c_ctx, mod_w, mod_b, norm1_g, norm2_g, w_in,
              mla_cq_g, mla_wuq, mla_ckv_g, mla_wukv, mla_qn_g, mla_kn_g, mla_wo,
              na_qn_g, na_kn_g, na_rpb, na_wo,
              rw_mu, rw_w0, rw_w2, rw_a0, rw_a2, rw_g2, rw_kk, rw_ka, rw_rk, rw_ln_g, rw_ln_b, rw_wo,
              w_out, ffn_w1, ffn_w3, ffn_w2, moe_router, moe_w1, moe_w3, moe_w2):
    B, L, D = x.shape
    n_ctx = ctx.shape[1]
    pos = jnp.arange(L)
    rows, cols = pos // GRID_W, pos % GRID_W
    s_lat = jax.nn.silu(c)
    s_ctx = jax.nn.silu(c_ctx)
    for l in range(DEPTH):
        need_ctx = l < DEPTH - 1
        mod = (s_lat @ mod_w[l] + mod_b[l])[:, None, :]
        mod_c = s_ctx @ mod_w[l] + mod_b[l]
        sh1, sc1, gt1, sh2, sc2, gt2 = jnp.split(mod, 6, axis=-1)
        csh1, csc1, cgt1, csh2, csc2, cgt2 = jnp.split(mod_c, 6, axis=-1)

        h = rms_norm(x, norm1_g[l]) * (1.0 + sc1) + sh1
        hc = rms_norm(ctx, norm1_g[l]) * (1.0 + csc1) + csh1
        m_cq, m_ckv, m_kr, n_q, n_k, n_v, r_z, gates = split_last(h @ w_in[l], IN_SIZES)
        mc_cq, mc_ckv, mc_kr, nc_q, nc_k, nc_v, rc_z, gates_c = split_last(hc @ w_in[l], IN_SIZES)

        qa = rope_2d_tail(mla_queries(m_cq, mla_cq_g[l], mla_wuq[l], mla_qn_g[l]), MLA_NOPE, rows, cols)
        ka, va = mla_keys_values(m_ckv, m_kr, mla_ckv_g[l], mla_wukv[l], mla_kn_g[l])
        ka = rope_2d_tail(ka, MLA_NOPE, rows, cols)
        ka_c, va_c = mla_keys_values(mc_ckv, mc_kr, mla_ckv_g[l], mla_wukv[l], mla_kn_g[l])
        ya = block_attention(qa, jnp.concatenate([ka_c, ka], axis=1),
                             jnp.concatenate([va_c, va], axis=1)).reshape(B, L, -1)

        qb = rms_norm(to_heads(n_q, NA_HEADS), na_qn_g[l])
        kb = rms_norm(to_heads(n_k, NA_HEADS), na_kn_g[l])
        vb = to_heads(n_v, NA_HEADS)
        kb_c = rms_norm(to_heads(nc_k, NA_HEADS), na_kn_g[l])
        vb_c = to_heads(nc_v, NA_HEADS)
        yb = neighbourhood_attention(qb, kb, vb, kb_c, vb_c, na_rpb[l])

        rw_p = (rw_mu[l], rw_w0[l], rw_w2[l], rw_a0[l], rw_a2[l], rw_kk[l], rw_ka[l])
        r_l, v_l, kk_l, w_l, a_l, kd_l, gd_l = rwkv_prepare(r_z, *rw_p)
        r_c, v_c, kk_c, w_c, a_c, kd_c, gd_c = rwkv_prepare(rc_z, *rw_p)
        s0 = jnp.zeros((B, RW_HEADS, RW_HEAD_DIM, RW_HEAD_DIM), jnp.float32)
        s_cf, y_cf = rwkv_scan(r_c, w_c[:, :, 0], kd_c[:, :, 0], v_c, kk_c, a_c[:, :, 0], s0, False)
        s_cb, y_cb = rwkv_scan(r_c, w_c[:, :, 1], kd_c[:, :, 1], v_c, kk_c, a_c[:, :, 1], s0, True)
        _, y_f = rwkv_scan(r_l, w_l[:, :, 0], kd_l[:, :, 0], v_l, kk_l, a_l[:, :, 0], s_cf, False)
        _, y_b = rwkv_scan(r_l, w_l[:, :, 1], kd_l[:, :, 1], v_l, kk_l, a_l[:, :, 1], s_cb, True)
        rw_o = (rw_g2[l], rw_rk[l], rw_ln_g[l], rw_ln_b[l])
        yr = rwkv_output(r_l, v_l, kd_l, gd_l, y_f + y_b, *rw_o, x.dtype)

        x_mid = x + gt1 * merge_branches(ya, yb, yr, gates, mla_wo[l], na_wo[l], rw_wo[l], w_out[l])
        if need_ctx:
            qa_c = mla_queries(mc_cq, mla_cq_g[l], mla_wuq[l], mla_qn_g[l])
            ya_c = block_attention(qa_c, ka_c, va_c).reshape(B, n_ctx, -1)
            qb_c = rms_norm(to_heads(nc_q, NA_HEADS), na_qn_g[l])
            yb_c = block_attention(qb_c, kb_c, vb_c).reshape(B, n_ctx, -1)
            yr_c = rwkv_output(r_c, v_c, kd_c, gd_c, y_cf + y_cb, *rw_o, ctx.dtype)
            ctx_mid = ctx + cgt1 * merge_branches(ya_c, yb_c, yr_c, gates_c, mla_wo[l], na_wo[l], rw_wo[l], w_out[l])

        tokens = (rms_norm(x_mid, norm2_g[l]) * (1.0 + sc2) + sh2).reshape(B * L, D)
        if need_ctx:
            h2c = rms_norm(ctx_mid, norm2_g[l]) * (1.0 + csc2) + csh2
            tokens = jnp.concatenate([tokens, h2c.reshape(B * n_ctx, D)], axis=0)
        if l % 2 == 0:
            f = swiglu(tokens, ffn_w1[l // 2], ffn_w3[l // 2], ffn_w2[l // 2])
        else:
            f = moe_swiglu(tokens, moe_router[l // 2], moe_w1[l // 2], moe_w3[l // 2], moe_w2[l // 2])
        x = x_mid + gt2 * f[:B * L].reshape(B, L, D)
        if need_ctx:
            ctx = ctx_mid + cgt2 * f[B * L:].reshape(B, n_ctx, D)
    return x
```

```python
import functools
import math

import numpy as np
import jax
import jax.numpy as jnp
from jax import lax
from jax.experimental import pallas as pl
from jax.experimental.pallas import tpu as pltpu

F32 = jnp.float32
BF16 = jnp.bfloat16

GRID_W = 64
MLA_HEADS, MLA_NOPE, MLA_ROPE, MLA_V = 8, 64, 32, 64
MLA_QK = MLA_NOPE + MLA_ROPE
MLA_Q_LORA, MLA_KV_LORA = 384, 256
NA_HEADS, NA_HEAD_DIM = 8, 64
NA_WIDTH = NA_HEADS * NA_HEAD_DIM
NA_WIN_ROWS, NA_WIN_COLS = 8, 16
RW_HEADS, RW_HEAD_DIM = 8, 64
RW_WIDTH = RW_HEADS * RW_HEAD_DIM
RW_DECAY_LORA, RW_ICLR_LORA, RW_GATE_LORA = 64, 64, 128
RW_GN_EPS = 64e-5
RW_IN = 3 * RW_WIDTH + 2 * RW_DECAY_LORA + 2 * RW_ICLR_LORA + RW_GATE_LORA
N_EXPERTS, TOP_K = 8, 2
ROPE_BASE = 10000.0
NORM_EPS = 1e-6

LANE = 128
SUBLANE = 8
VMEM_LIMIT = 56 * 1024 * 1024

TM = 256
HEAD_PAD = LANE
RW_CHUNK = 64
RW_GROUP = 4
NA_ROWS = 4
NA_KROWS = NA_ROWS + NA_WIN_ROWS
MOE_BM = 512
MOE_FC = 512
FFN_FC = 256
NEG_BIG = -1e30


def _cp(n_axes, vmem=VMEM_LIMIT):
    return pltpu.CompilerParams(dimension_semantics=("arbitrary",) * n_axes, vmem_limit_bytes=vmem)


def _dot(a, b):
    return jnp.dot(a, b, preferred_element_type=F32)


def _dot_nt(a, b):
    return lax.dot_general(a, b, (((1,), (1,)), ((), ())), preferred_element_type=F32)


def _dot_tn(a, b):
    return lax.dot_general(a, b, (((0,), (0,)), ((), ())), preferred_element_type=F32)


def _split2(x):
    hi = x.astype(BF16)
    lo = (x - hi.astype(F32)).astype(BF16)
    return hi, lo


def _split3(x):
    hi = x.astype(BF16)
    r1 = x - hi.astype(F32)
    mid = r1.astype(BF16)
    lo = (r1 - mid.astype(F32)).astype(BF16)
    return hi, mid, lo


def _seg_sum(x, seg01):
    hi, lo = _split2(x)
    return _dot(hi, seg01) + _dot(lo, seg01)


def _full(shape):
    nd = len(shape)
    return pl.BlockSpec(shape, lambda *_: (0,) * nd)


def _modulated_norm(x, g, mod, first_row, n_ctx, which):
    d = x.shape[1]
    off = 0 if which == 1 else 3 * d
    y = x * lax.rsqrt(jnp.mean(x * x, axis=-1, keepdims=True) + NORM_EPS) * g
    rows = first_row + lax.broadcasted_iota(jnp.int32, (x.shape[0], 1), 0)
    is_ctx = rows < n_ctx
    sh = jnp.where(is_ctx, mod[1:2, off:off + d], mod[0:1, off:off + d])
    sc = jnp.where(is_ctx, mod[1:2, off + d:off + 2 * d], mod[0:1, off + d:off + 2 * d])
    return y * (1.0 + sc) + sh


def _gate_rows(mod, first_row, n_rows, n_ctx, which):
    d = mod.shape[1] // 6
    off = 2 * d if which == 1 else 5 * d
    rows = first_row + lax.broadcasted_iota(jnp.int32, (n_rows, 1), 0)
    return jnp.where(rows < n_ctx, mod[1:2, off:off + d], mod[0:1, off:off + d])


def _mod_body(cc_ref, w_ref, b_ref, o_ref):
    cc = cc_ref[...]
    s = cc * jax.nn.sigmoid(cc)
    o_ref[0] = _dot(s.astype(BF16), w_ref[0].astype(BF16)) + b_ref[0]


def _modulation(c, c_ctx, mod_w, mod_b):
    depth, d, d6 = mod_w.shape
    nt = d6 // 4
    cc = jnp.zeros((SUBLANE, d), F32).at[0].set(c[0]).at[1].set(c_ctx)
    return pl.pallas_call(
        _mod_body,
        grid=(depth, d6 // nt),
        in_specs=[_full((SUBLANE, d)),
                  pl.BlockSpec((1, d, nt), lambda l, j: (l, 0, j)),
                  pl.BlockSpec((1, 1, nt), lambda l, j: (l, 0, j))],
        out_specs=pl.BlockSpec((1, SUBLANE, nt), lambda l, j: (l, 0, j)),
        out_shape=jax.ShapeDtypeStruct((depth, SUBLANE, d6), F32),
        compiler_params=_cp(2),
        name="modulation",
    )(cc, mod_w, mod_b.reshape(depth, 1, d6))


def _rms(x, width):
    return x * lax.rsqrt(jnp.sum(x * x, axis=-1, keepdims=True) * (1.0 / width) + NORM_EPS)


def _rope(x, cos, sin_a, sin_b):
    return x * cos + pltpu.roll(x, HEAD_PAD - 8, 1) * sin_a + pltpu.roll(x, 8, 1) * sin_b


def _mla_proj_body(n_ctx, x_ref, mod_ref, g1_ref, wa_ref, cqg_ref, wuq_ref, ckvg_ref, wuk_ref, wuvt_ref,
                   qng_ref, kng_ref, cos_ref, sa_ref, sb_ref, q_ref, k_ref, vt_ref):
    tm = x_ref.shape[0]
    h = _modulated_norm(x_ref[...], g1_ref[...], mod_ref[...], pl.program_id(0) * tm, n_ctx, 1).astype(BF16)
    z = _dot(h, wa_ref[...])
    cq = z[:, :MLA_Q_LORA]
    ckv = z[:, MLA_Q_LORA:MLA_Q_LORA + MLA_KV_LORA]
    kr = z[:, MLA_Q_LORA + MLA_KV_LORA:]
    cqn = (_rms(cq, MLA_Q_LORA) * cqg_ref[...]).astype(BF16)
    ckvn = (_rms(ckv, MLA_KV_LORA) * ckvg_ref[...]).astype(BF16)
    q = _dot(cqn, wuq_ref[...])
    kn = _dot(ckvn, wuk_ref[...])
    vt_ref[0] = _dot_nt(wuvt_ref[...], ckvn).astype(BF16)
    kr_sh = pltpu.roll(kr, MLA_NOPE, 1)
    cos, sa, sb = cos_ref[...], sa_ref[...], sb_ref[...]
    scale = MLA_QK ** -0.5
    for hh in range(MLA_HEADS):
        sl = slice(hh * HEAD_PAD, (hh + 1) * HEAD_PAD)
        qh = _rope(_rms(q[:, sl], MLA_QK) * qng_ref[...], cos, sa, sb) * scale
        kh = _rope(_rms(kn[:, sl] + kr_sh, MLA_QK) * kng_ref[...], cos, sa, sb)
        q_ref[:, sl] = qh.astype(BF16)
        k_ref[:, sl] = kh.astype(BF16)


def _mla_proj(x_all, mod_l, g1, wts, rope_tabs, n_ctx):
    t, d = x_all.shape
    nt = t // TM
    hp = MLA_HEADS * HEAD_PAD
    wa, cqg, wuq, ckvg, wuk, wuvt, qng, kng = wts
    tok = lambda w: pl.BlockSpec((TM, w), lambda i: (i, 0))
    return pl.pallas_call(
        functools.partial(_mla_proj_body, n_ctx),
        grid=(nt,),
        in_specs=[tok(d), _full(mod_l.shape), _full(g1.shape), _full(wa.shape), _full(cqg.shape), _full(wuq.shape),
                  _full(ckvg.shape), _full(wuk.shape), _full(wuvt.shape), _full(qng.shape), _full(kng.shape),
                  tok(HEAD_PAD), tok(HEAD_PAD), tok(HEAD_PAD)],
        out_specs=[tok(hp), tok(hp), pl.BlockSpec((1, MLA_HEADS * MLA_V, TM), lambda i: (i, 0, 0))],
        out_shape=[jax.ShapeDtypeStruct((t, hp), BF16), jax.ShapeDtypeStruct((t, hp), BF16),
                   jax.ShapeDtypeStruct((nt, MLA_HEADS * MLA_V, TM), BF16)],
        compiler_params=_cp(1),
        name="mla_proj",
    )(x_all, mod_l, g1, wa, cqg, wuq, ckvg, wuk, wuvt, qng, kng, *rope_tabs)


def _mla_attn_body(n_ctx_tiles, q_off, q_ref, k_ref, vt_ref, o_ref):
    tq = q_ref.shape[0]
    nk, _, tk = vt_ref.shape
    q = q_ref[...]
    qi = pl.program_id(1) + q_off
    nk_eff = jnp.where(qi < n_ctx_tiles, n_ctx_tiles, nk)

    def step(j, carry):
        m, l, acc = carry
        kj = k_ref[pl.ds(pl.multiple_of(j * tk, tk), tk), :]
        s = _dot_nt(kj, q)
        m_new = jnp.maximum(m, jnp.max(s, axis=0, keepdims=True))
        alpha = jnp.exp(m - m_new)
        p = jnp.exp(s - m_new)
        l = alpha * l + jnp.sum(p, axis=0, keepdims=True)
        acc = alpha * acc + _dot(vt_ref[j], p.astype(BF16))
        return m_new, l, acc

    init = (jnp.full((1, tq), NEG_BIG, F32), jnp.zeros((1, tq), F32), jnp.zeros((MLA_V, tq), F32))
    _, l, acc = lax.fori_loop(0, nk_eff, step, init)
    o_ref[...] = acc / l


def _mla_attn(q, k, vt, n_ctx, with_ctx):
    t = q.shape[0]
    nk = vt.shape[0]
    tq = TM
    q_off = 0 if with_ctx else n_ctx // tq
    nq = t // tq - q_off
    return pl.pallas_call(
        functools.partial(_mla_attn_body, n_ctx // TM, q_off),
        grid=(MLA_HEADS, nq),
        in_specs=[pl.BlockSpec((tq, HEAD_PAD), lambda h, i: (i + q_off, h)),
                  pl.BlockSpec((t, HEAD_PAD), lambda h, i: (0, h)),
                  pl.BlockSpec((nk, MLA_V, TM), lambda h, i: (0, h, 0))],
        out_specs=pl.BlockSpec((MLA_V, tq), lambda h, i: (h, i)),
        out_shape=jax.ShapeDtypeStruct((MLA_HEADS * MLA_V, nq * tq), F32),
        compiler_params=_cp(2),
        name="mla_attn",
    )(q, k, vt)


def _na_proj_body(n_ctx, x_ref, mod_ref, g1_ref, w_ref, qg_ref, kg_ref, seg_ref, q_ref, k_ref, v_ref):
    tm = x_ref.shape[0]
    h = _modulated_norm(x_ref[...], g1_ref[...], mod_ref[...], pl.program_id(0) * tm, n_ctx, 1).astype(BF16)
    z = _dot(h, w_ref[...])
    seg = seg_ref[...]
    inv = 1.0 / NA_HEAD_DIM

    def head_norm(y, g):
        return y * lax.rsqrt(_seg_sum(y * y, seg) * inv + NORM_EPS) * g

    q_ref[...] = (head_norm(z[:, :NA_WIDTH], qg_ref[...]) * (NA_HEAD_DIM ** -0.5)).astype(BF16)
    k_ref[...] = head_norm(z[:, NA_WIDTH:2 * NA_WIDTH], kg_ref[...]).astype(BF16)
    v_ref[...] = z[:, 2 * NA_WIDTH:].astype(BF16)


def _na_proj(x_all, mod_l, g1, w_na, qg, kg, seg, n_ctx):
    t, d = x_all.shape
    tok = lambda w: pl.BlockSpec((TM, w), lambda i: (i, 0))
    return pl.pallas_call(
        functools.partial(_na_proj_body, n_ctx),
        grid=(t // TM,),
        in_specs=[tok(d), _full(mod_l.shape), _full(g1.shape), _full(w_na.shape), _full(qg.shape), _full(kg.shape),
                  _full(seg.shape)],
        out_specs=[tok(NA_WIDTH)] * 3,
        out_shape=[jax.ShapeDtypeStruct((t, NA_WIDTH), BF16)] * 3,
        compiler_params=_cp(1),
        name="na_proj",
    )(x_all, mod_l, g1, w_na, qg, kg, seg)


def _head_masks():
    lane = lax.broadcasted_iota(jnp.int32, (1, LANE), 1)
    return [(lane // NA_HEAD_DIM == h) for h in range(LANE // NA_HEAD_DIM)]


def _na_attn_body(n_ctx, n_grid_rows, q_ref, k_ref, v_ref, bias_ref, o_ref):
    nq = q_ref.shape[0]
    nkl = NA_KROWS * GRID_W
    g = pl.program_id(1)
    kr0 = jnp.clip(g * NA_ROWS - NA_WIN_ROWS // 2, 0, n_grid_rows - NA_KROWS)
    start = pl.multiple_of(n_ctx + kr0 * GRID_W, GRID_W)
    k_loc = k_ref[pl.ds(start, nkl), :]
    v_loc = v_ref[pl.ds(start, nkl), :]
    k_ctx = k_ref[0:n_ctx, :]
    v_ctx = v_ref[0:n_ctx, :]
    q = q_ref[...]
    out = jnp.zeros((nq, LANE), F32)
    for h, hm in enumerate(_head_masks()):
        qh = jnp.where(hm, q, jnp.zeros_like(q))
        s_loc = _dot_nt(qh, k_loc) + bias_ref[0, h]
        s_ctx = _dot_nt(qh, k_ctx)
        m = jnp.maximum(jnp.max(s_loc, axis=-1, keepdims=True), jnp.max(s_ctx, axis=-1, keepdims=True))
        p_loc = jnp.exp(s_loc - m)
        p_ctx = jnp.exp(s_ctx - m)
        l = jnp.sum(p_loc, axis=-1, keepdims=True) + jnp.sum(p_ctx, axis=-1, keepdims=True)
        o = (_dot(p_loc.astype(BF16), v_loc) + _dot(p_ctx.astype(BF16), v_ctx)) / l
        out = jnp.where(hm, o, out)
    o_ref[...] = out


def _na_attn(q, k, v, bias, n_ctx):
    t = q.shape[0]
    n_lat = t - n_ctx
    n_grid_rows = n_lat // GRID_W
    nq = NA_ROWS * GRID_W
    ng = n_grid_rows // NA_ROWS
    q_blk0 = n_ctx // nq
    npairs = NA_WIDTH // LANE

    def case(g):
        return jnp.where(g == 0, 0, jnp.where(g == ng - 1, 2, 1))

    return pl.pallas_call(
        functools.partial(_na_attn_body, n_ctx, n_grid_rows),
        grid=(npairs, ng),
        in_specs=[pl.BlockSpec((nq, LANE), lambda p, g: (g + q_blk0, p)),
                  pl.BlockSpec((t, LANE), lambda p, g: (0, p)),
                  pl.BlockSpec((t, LANE), lambda p, g: (0, p)),
                  pl.BlockSpec((1, LANE // NA_HEAD_DIM, nq, NA_KROWS * GRID_W), lambda p, g: (case(g), p, 0, 0))],
        out_specs=pl.BlockSpec((nq, LANE), lambda p, g: (g, p)),
        out_shape=jax.ShapeDtypeStruct((n_lat, NA_WIDTH), F32),
        compiler_params=_cp(2),
        name="na_attn",
    )(q, k, v, bias)


def _ctx_attn_body(q_ref, k_ref, v_ref, o_ref):
    q, k, v = q_ref[...], k_ref[...], v_ref[...]
    out = jnp.zeros(o_ref.shape, F32)
    for hm in _head_masks():
        s = _dot_nt(jnp.where(hm, q, jnp.zeros_like(q)), k)
        p = jnp.exp(s - jnp.max(s, axis=-1, keepdims=True))
        o = _dot(p.astype(BF16), v) / jnp.sum(p, axis=-1, keepdims=True)
        out = jnp.where(hm, o, out)
    o_ref[...] = out


def _ctx_attn(q, k, v, n_ctx):
    npairs = NA_WIDTH // LANE
    blk = pl.BlockSpec((n_ctx, LANE), lambda p: (0, p))
    return pl.pallas_call(
        _ctx_attn_body,
        grid=(npairs,),
        in_specs=[blk, blk, blk],
        out_specs=blk,
        out_shape=jax.ShapeDtypeStruct((n_ctx, NA_WIDTH), F32),
        compiler_params=_cp(1),
        name="ctx_attn",
    )(q, k, v)


def _na_bias_tables(rpb, n_grid_rows):
    qr = np.arange(NA_ROWS)[:, None, None, None]
    qc = np.arange(GRID_W)[None, :, None, None]
    kr = np.arange(NA_KROWS)[None, None, :, None]
    kc = np.arange(GRID_W)[None, None, None, :]
    col_start = np.clip(qc - NA_WIN_COLS // 2, 0, GRID_W - NA_WIN_COLS)
    col_ok = (kc >= col_start) & (kc < col_start + NA_WIN_COLS)
    dc = np.clip(kc - qc + NA_WIN_COLS - 1, 0, 2 * NA_WIN_COLS - 2)
    tabs = []
    for off, win_start in ((0, np.zeros_like(qr)), (NA_WIN_ROWS // 2, qr), (NA_WIN_ROWS, np.full_like(qr, NA_ROWS))):
        row_ok = (kr >= win_start) & (kr < win_start + NA_WIN_ROWS)
        dr = np.clip(kr - off - qr + NA_WIN_ROWS - 1, 0, 2 * NA_WIN_ROWS - 2)
        ok = np.broadcast_to(row_ok & col_ok, (NA_ROWS, GRID_W, NA_KROWS, GRID_W)).reshape(NA_ROWS * GRID_W, NA_KROWS * GRID_W)
        dr_b = np.broadcast_to(dr, (NA_ROWS, GRID_W, NA_KROWS, GRID_W)).reshape(ok.shape)
        dc_b = np.broadcast_to(dc, (NA_ROWS, GRID_W, NA_KROWS, GRID_W)).reshape(ok.shape)
        tabs.append(jnp.where(ok[None], rpb[:, dr_b, dc_b].astype(F32), NEG_BIG))
    return jnp.stack(tabs)


def _rz_proj_body(n_ctx, x_ref, mod_ref, g1_ref, w_ref, z_ref):
    tm = x_ref.shape[0]
    h = _modulated_norm(x_ref[...], g1_ref[...], mod_ref[...], pl.program_id(0) * tm, n_ctx, 1).astype(BF16)
    z_ref[...] = _dot(h, w_ref[...])


def _rz_proj(x_all, mod_l, g1, w_rz, n_ctx):
    t, d = x_all.shape
    tok = lambda w: pl.BlockSpec((TM, w), lambda i: (i, 0))
    return pl.pallas_call(
        functools.partial(_rz_proj_body, n_ctx),
        grid=(t // TM,),
        in_specs=[tok(d), _full(mod_l.shape), _full(g1.shape), _full(w_rz.shape)],
        out_specs=tok(RW_IN),
        out_shape=jax.ShapeDtypeStruct((t, RW_IN), F32),
        compiler_params=_cp(1),
        name="rz_proj",
    )(x_all, mod_l, g1, w_rz)


def _rw_prep_body(n_ctx, n_tok, z_ref, zp_ref, zn_ref, mu_ref, w0_ref, w2_ref, a0_ref, a2_ref, kk_ref, ka_ref, seg_ref,
                  r_ref, v_ref, kkn_ref, lw_ref, beta_ref, kd_ref, gd_ref):
    tm = z_ref.shape[0]
    z = z_ref[...]
    loc = lax.broadcasted_iota(jnp.int32, (tm, 1), 0)
    rows = pl.program_id(0) * tm + loc
    zp = jnp.where(loc == 0, zp_ref[0][SUBLANE - 1:SUBLANE, :], pltpu.roll(z, 1, 0))
    zp = jnp.where((rows == 0) | (rows == n_ctx), 0.0, zp)
    zn = jnp.where(loc == tm - 1, zn_ref[0][0:1, :], pltpu.roll(z, tm - 1, 0))
    zn = jnp.where((rows == n_ctx - 1) | (rows == n_tok - 1), 0.0, zn)
    mu = mu_ref[...]
    zs = z + mu[0:1] * (zp - z) + mu[1:2] * (zn - z)
    w = RW_WIDTH
    r, k, v = zs[:, :w], zs[:, w:2 * w], zs[:, 2 * w:3 * w]
    wd = zs[:, 3 * w:3 * w + 2 * RW_DECAY_LORA]
    ad = zs[:, 3 * w + 2 * RW_DECAY_LORA:3 * w + 2 * RW_DECAY_LORA + 2 * RW_ICLR_LORA]
    gd_ref[...] = zs[:, RW_IN - RW_GATE_LORA:]
    u = w0_ref[...] + _dot(jnp.tanh(wd).astype(BF16), w2_ref[...])
    a = jax.nn.sigmoid(a0_ref[...] + _dot(ad.astype(BF16), a2_ref[...]))
    lw = -jax.nn.sigmoid(u) * math.exp(-0.5)
    kkr = k * kk_ref[...]
    kkn = kkr / jnp.maximum(jnp.sqrt(_seg_sum(kkr * kkr, seg_ref[...])), 1e-12)
    r_ref[...] = r
    v_ref[...] = v
    kkn_ref[...] = kkn
    for d in range(2):
        a_d = a[:, d * w:(d + 1) * w]
        lw_ref[d] = lw[:, d * w:(d + 1) * w]
        beta_ref[d] = kkn * a_d
        kd_ref[d] = k * (1.0 + (a_d - 1.0) * ka_ref[...])


def _rw_prep(z, mu, w0, w2b, a0, a2b, k_k, k_a, seg, n_ctx):
    t = z.shape[0]
    n8 = TM // SUBLANE
    z3 = z.reshape(t // SUBLANE, SUBLANE, RW_IN)
    tok = lambda w: pl.BlockSpec((TM, w), lambda i: (i, 0))
    tok2 = pl.BlockSpec((2, TM, RW_WIDTH), lambda i: (0, i, 0))
    f = jax.ShapeDtypeStruct((t, RW_WIDTH), F32)
    f2 = jax.ShapeDtypeStruct((2, t, RW_WIDTH), F32)
    return pl.pallas_call(
        functools.partial(_rw_prep_body, n_ctx, t),
        grid=(t // TM,),
        in_specs=[tok(RW_IN),
                  pl.BlockSpec((1, SUBLANE, RW_IN), lambda i: (jnp.maximum(i * n8 - 1, 0), 0, 0)),
                  pl.BlockSpec((1, SUBLANE, RW_IN), lambda i: (jnp.minimum((i + 1) * n8, t // SUBLANE - 1), 0, 0)),
                  _full(mu.shape), _full(w0.shape), _full(w2b.shape), _full(a0.shape), _full(a2b.shape),
                  _full(k_k.shape), _full(k_a.shape), _full(seg.shape)],
        out_specs=[tok(RW_WIDTH), tok(RW_WIDTH), tok(RW_WIDTH), tok2, tok2, tok2, tok(RW_GATE_LORA)],
        out_shape=[f, f, f, f2, f2, f2, jax.ShapeDtypeStruct((t, RW_GATE_LORA), F32)],
        compiler_params=_cp(1),
        name="rw_prep",
    )(z, z3, z3, mu, w0, w2b, a0, a2b, k_k, k_a, seg)


def _rw_group_step(r, v, kkn, lw, beta, kd, s_bf, reverse):
    c, gl = r.shape
    n = gl // RW_GROUP
    t_i = lax.broadcasted_iota(jnp.int32, (c, c), 0)
    s_i = lax.broadcasted_iota(jnp.int32, (c, c), 1)
    incl = (s_i >= t_i) if reverse else (s_i <= t_i)
    h1, h2, h3 = _split3(lw)
    tri = incl.astype(BF16)
    g_inc = _dot(tri, h1) + _dot(tri, h2) + _dot(tri, h3)
    e_pos = jnp.exp(g_inc)
    e_neg = jnp.exp(-g_inc)
    a_t = (-kkn * jnp.exp(g_inc - lw)).astype(BF16)
    b_t = (beta * e_neg).astype(BF16)
    k_t = (kd * e_neg).astype(BF16)
    r_t = (r * e_pos).astype(BF16)
    decay_c = jnp.exp(jnp.sum(lw, axis=0, keepdims=True))

    row_blk = lax.broadcasted_iota(jnp.int32, (RW_GROUP * c, gl), 0) // c
    lane_blk = lax.broadcasted_iota(jnp.int32, (RW_GROUP * c, gl), 1) // n
    diag_blk = row_blk == lane_blk

    def x4(m):
        return jnp.where(diag_blk, jnp.concatenate([m] * RW_GROUP, axis=0), jnp.zeros((), m.dtype))

    tt = lax.broadcasted_iota(jnp.int32, (c, RW_GROUP * c), 0)
    ss = lax.broadcasted_iota(jnp.int32, (c, RW_GROUP * c), 1) % c
    strict = (ss > tt) if reverse else (ss < tt)
    incl4 = (ss >= tt) if reverse else (ss <= tt)
    eye4 = ss == tt

    xb, xk = x4(b_t), x4(k_t)
    l_ab = jnp.where(strict, _dot_nt(a_t, xb), 0.0)
    l_ak = jnp.where(strict, _dot_nt(a_t, xk), 0.0)
    m_rb = jnp.where(incl4, _dot_nt(r_t, xb), 0.0).astype(BF16)
    m_rk = jnp.where(incl4, _dot_nt(r_t, xk), 0.0).astype(BF16)

    p = l_ab
    tinv = jnp.where(eye4, 1.0, 0.0) + l_ab
    k2 = 1
    while 2 * k2 < c:
        p = _dot(p.astype(BF16), x4(p.astype(BF16)))
        tinv = tinv + _dot(tinv.astype(BF16), x4(p.astype(BF16)))
        k2 *= 2
    tinv = tinv.astype(BF16)

    v_bf = v.astype(BF16)
    xv = x4(v_bf)
    wm = _dot(tinv, x4(a_t)).astype(BF16)
    u0 = _dot(tinv, x4(_dot(l_ak.astype(BF16), xv).astype(BF16)))
    u = _dot_nt(wm, s_bf) + u0
    u_bf = u.astype(BF16)
    y = _dot_nt(r_t, s_bf) + _dot(m_rb, x4(u_bf)) + _dot(m_rk, xv)
    ds = _dot_tn(u_bf, b_t) + _dot_tn(v_bf, k_t)
    return y, ds, decay_c


def _rw_chunk_body(nc_ctx, rf_ref, vf_ref, kf_ref, lwf_ref, bf_ref, kdf_ref,
                   rb_ref, vb_ref, kb_ref, lwb_ref, bb_ref, kdb_ref, yf_ref, yb_ref, s_ref):
    @pl.when(pl.program_id(0) == 0)
    def _():
        s_ref[...] = jnp.zeros_like(s_ref)

    gl = RW_GROUP * RW_HEAD_DIM
    blk = (lax.broadcasted_iota(jnp.int32, (gl, gl), 0) // RW_HEAD_DIM
           == lax.broadcasted_iota(jnp.int32, (gl, gl), 1) // RW_HEAD_DIM)
    dirs = ((rf_ref, vf_ref, kf_ref, lwf_ref, bf_ref, kdf_ref, yf_ref, False),
            (rb_ref, vb_ref, kb_ref, lwb_ref, bb_ref, kdb_ref, yb_ref, True))
    for d, (r_ref, v_ref, k_ref, lw_ref, b_ref, kd_ref, y_ref, rev) in enumerate(dirs):
        for g in range(RW_WIDTH // gl):
            sl = slice(g * gl, (g + 1) * gl)
            idx = d * (RW_WIDTH // gl) + g
            s = s_ref[idx]
            y, ds, dec = _rw_group_step(r_ref[:, sl], v_ref[:, sl], k_ref[:, sl], lw_ref[0][:, sl], b_ref[0][:, sl],
                                        kd_ref[0][:, sl], s.astype(BF16), rev)
            y_ref[:, sl] = y
            s_ref[idx] = (s + jnp.where(blk, ds, 0.0)) * dec


def _rw_chunks(r, v, kkn, lw, beta, kd, n_ctx):
    t = r.shape[0]
    c = RW_CHUNK
    nc = t // c
    nc_ctx = n_ctx // c

    def fwd(i):
        return i

    def bwd(i):
        return jnp.where(i < nc_ctx, nc_ctx - 1 - i, nc - 1 - (i - nc_ctx))

    def specs(m):
        one = pl.BlockSpec((c, RW_WIDTH), lambda i: (m(i), 0))
        return one

    def specs2(m, d):
        return pl.BlockSpec((1, c, RW_WIDTH), lambda i: (d, m(i), 0))

    in_specs = [specs(fwd), specs(fwd), specs(fwd), specs2(fwd, 0), specs2(fwd, 0), specs2(fwd, 0),
                specs(bwd), specs(bwd), specs(bwd), specs2(bwd, 1), specs2(bwd, 1), specs2(bwd, 1)]
    f = jax.ShapeDtypeStruct((t, RW_WIDTH), F32)
    n_state = 2 * RW_HEADS // RW_GROUP
    gl = RW_GROUP * RW_HEAD_DIM
    return pl.pallas_call(
        functools.partial(_rw_chunk_body, nc_ctx),
        grid=(nc,),
        in_specs=in_specs,
        out_specs=[specs(fwd), specs(bwd)],
        out_shape=[f, f],
        scratch_shapes=[pltpu.VMEM((n_state, gl, gl), F32)],
        compiler_params=_cp(1),
        name="rw_chunks",
    )(r, v, kkn, lw, beta, kd, r, v, kkn, lw, beta, kd)


def _merge_body(n_ctx, t_off, x_ref, mod_ref, g1_ref, g2n_ref, wg_ref, yat_ref, yb_ref, yf_ref, ybw_ref, r_ref, v_ref,
                kd_ref, gd_ref, g2_ref, rk_ref, lng_ref, lnb_ref, seg_ref, woa_ref, wob_ref, wor_ref, wout_ref,
                xmid_ref, h2_ref):
    tm, d = x_ref.shape
    row0 = (pl.program_id(0) + t_off) * tm
    x = x_ref[...]
    mod = mod_ref[...]
    h = _modulated_norm(x, g1_ref[...], mod, row0, n_ctx, 1).astype(BF16)
    gates = jax.nn.sigmoid(_dot(h, wg_ref[...]))
    pa = _dot_tn(yat_ref[...].astype(BF16), woa_ref[...])
    pb = _dot(yb_ref[...].astype(BF16), wob_ref[...])
    seg = seg_ref[...]
    inv = 1.0 / RW_HEAD_DIM
    y = yf_ref[...] + ybw_ref[...]
    dy = y - _seg_sum(y, seg) * inv
    yn = dy * lax.rsqrt(_seg_sum(dy * dy, seg) * inv + RW_GN_EPS) * lng_ref[...] + lnb_ref[...]
    bonus = _seg_sum(r_ref[...] * (kd_ref[0] + kd_ref[1]) * rk_ref[...], seg) * v_ref[...]
    g = _dot(jax.nn.sigmoid(gd_ref[...]).astype(BF16), g2_ref[...])
    pr = _dot(((yn + bonus) * g).astype(BF16), wor_ref[...])
    m = gates[:, :d] * pa + gates[:, d:2 * d] * pb + gates[:, 2 * d:] * pr
    o = _dot(m.astype(BF16), wout_ref[...])
    x_mid = x + _gate_rows(mod, row0, tm, n_ctx, 1) * o
    xmid_ref[...] = x_mid
    h2_ref[...] = _modulated_norm(x_mid, g2n_ref[...], mod, row0, n_ctx, 2).astype(h2_ref.dtype)


def _merge(x_all, mod_l, g1, g2n, wg, yat, yb, yf, ybw, r, v, kd, gd, g2, rk, lng, lnb, seg, woa, wob, wor, wout,
           n_ctx, with_ctx, h2_dtype):
    t, d = x_all.shape
    t_off = 0 if with_ctx else n_ctx // TM
    n_rows = yat.shape[1]
    tok = lambda w: pl.BlockSpec((TM, w), lambda i: (i + t_off, 0))
    own = lambda w: pl.BlockSpec((TM, w), lambda i: (i, 0))
    smalls = (g2, rk, lng, lnb, seg, woa, wob, wor, wout)
    return pl.pallas_call(
        functools.partial(_merge_body, n_ctx, t_off),
        grid=(n_rows // TM,),
        in_specs=[tok(d), _full(mod_l.shape), _full(g1.shape), _full(g2n.shape), _full(wg.shape),
                  pl.BlockSpec((yat.shape[0], TM), lambda i: (0, i)), own(NA_WIDTH),
                  tok(RW_WIDTH), tok(RW_WIDTH), tok(RW_WIDTH), tok(RW_WIDTH),
                  pl.BlockSpec((2, TM, RW_WIDTH), lambda i: (0, i + t_off, 0)), tok(RW_GATE_LORA)]
                 + [_full(a.shape) for a in smalls],
        out_specs=[own(d), own(d)],
        out_shape=[jax.ShapeDtypeStruct((n_rows, d), F32), jax.ShapeDtypeStruct((n_rows, d), h2_dtype)],
        compiler_params=_cp(1),
        name="merge",
    )(x_all, mod_l, g1, g2n, wg, yat, yb, yf, ybw, r, v, kd, gd, *smalls)


def _ffn_body(n_ctx, h_ref, x_ref, mod_ref, w1_ref, w3_ref, w2_ref, o_ref):
    tm = h_ref.shape[0]
    h = h_ref[...]
    acc = jnp.zeros(o_ref.shape, F32)
    for f in range(w1_ref.shape[0]):
        a = _dot(h, w1_ref[f])
        b = _dot(h, w3_ref[f])
        acc = acc + _dot((a * jax.nn.sigmoid(a) * b).astype(BF16), w2_ref[f])
    o_ref[...] = x_ref[...] + _gate_rows(mod_ref[...], pl.program_id(0) * tm, tm, n_ctx, 2) * acc


def _ffn(h2, x_mid, mod_l, w1, w3, w2, n_ctx):
    t, d = x_mid.shape
    tok = pl.BlockSpec((TM, d), lambda i: (i, 0))
    return pl.pallas_call(
        functools.partial(_ffn_body, n_ctx),
        grid=(t // TM,),
        in_specs=[tok, tok, _full(mod_l.shape), _full(w1.shape), _full(w3.shape), _full(w2.shape)],
        out_specs=tok,
        out_shape=jax.ShapeDtypeStruct((t, d), F32),
        compiler_params=_cp(1),
        name="ffn",
    )(h2, x_mid, mod_l, w1, w3, w2)


def _router_body(h_ref, w_ref, idx_ref, gate_ref):
    logits = _dot(h_ref[...].astype(BF16), w_ref[...])
    lane_i = lax.broadcasted_iota(jnp.int32, logits.shape, 1)
    lane = lane_i.astype(F32)
    logits = jnp.where(lane_i < N_EXPERTS, logits, NEG_BIG)
    m1 = jnp.max(logits, axis=-1, keepdims=True)
    i1 = jnp.min(jnp.where(logits == m1, lane, float(LANE)), axis=-1, keepdims=True)
    rest = jnp.where(lane == i1, NEG_BIG, logits)
    m2 = jnp.max(rest, axis=-1, keepdims=True)
    i2 = jnp.min(jnp.where(rest == m2, lane, float(LANE)), axis=-1, keepdims=True)
    e2 = jnp.exp(m2 - m1)
    g1 = 1.0 / (1.0 + e2)
    lane = lane_i
    idx_ref[...] = jnp.where(lane == 0, i1, jnp.where(lane == 1, i2, 0.0)).astype(jnp.int32)
    gate_ref[...] = jnp.where(lane == 0, g1, jnp.where(lane == 1, e2 * g1, 0.0))


def _router(h2, w_router_pad):
    n, d = h2.shape
    tok = lambda w: pl.BlockSpec((TM, w), lambda i: (i, 0))
    return pl.pallas_call(
        _router_body,
        grid=(n // TM,),
        in_specs=[tok(d), _full(w_router_pad.shape)],
        out_specs=[tok(LANE), tok(LANE)],
        out_shape=[jax.ShapeDtypeStruct((n, LANE), jnp.int32), jax.ShapeDtypeStruct((n, LANE), F32)],
        compiler_params=_cp(1),
        name="router",
    )(h2, w_router_pad)


def _row_copy(src_hbm, dst_vmem, src_row, dst_row, sem):
    return pltpu.make_async_copy(src_hbm.at[pl.ds(src_row, 1)], dst_vmem.at[pl.ds(dst_row, 1)], sem)


def _gather_body(idx_hbm, src_hbm, o_ref, idx_smem, sem_idx, sem_rows):
    bm = o_ref.shape[0]
    cp = pltpu.make_async_copy(idx_hbm.at[pl.program_id(0)], idx_smem, sem_idx)
    cp.start()
    cp.wait()

    def start(r, carry):
        _row_copy(src_hbm, o_ref, idx_smem[r], r, sem_rows).start()
        return carry

    lax.fori_loop(0, bm, start, 0)

    def wait(r, carry):
        _row_copy(src_hbm, o_ref, 0, r, sem_rows).wait()
        return carry

    lax.fori_loop(0, bm, wait, 0)


def _gather_rows(idx, src, bm):
    nb = idx.shape[0]
    width = src.shape[1]
    return pl.pallas_call(
        _gather_body,
        grid=(nb,),
        in_specs=[pl.BlockSpec(memory_space=pl.ANY), pl.BlockSpec(memory_space=pl.ANY)],
        out_specs=pl.BlockSpec((bm, width), lambda b: (b, 0)),
        out_shape=jax.ShapeDtypeStruct((nb * bm, width), src.dtype),
        scratch_shapes=[pltpu.SMEM((bm,), jnp.int32), pltpu.SemaphoreType.DMA(()), pltpu.SemaphoreType.DMA(())],
        compiler_params=_cp(1),
        name="gather_rows",
    )(idx, src)


def _moe_ffn_body(be_ref, nb_ref, x_ref, w1_ref, w3_ref, w2_ref, o_ref, acc_ref):
    b, f = pl.program_id(0), pl.program_id(1)
    nf = pl.num_programs(1)

    @pl.when(f == 0)
    def _():
        acc_ref[...] = jnp.zeros_like(acc_ref)

    @pl.when(b < nb_ref[0])
    def _():
        x = x_ref[...].astype(BF16)
        a = _dot(x, w1_ref[0])
        g = _dot(x, w3_ref[0])
        acc_ref[...] += _dot((a * jax.nn.sigmoid(a) * g).astype(BF16), w2_ref[0])

    @pl.when(f == nf - 1)
    def _():
        o_ref[...] = acc_ref[...]


def _moe_ffn(blk_expert, n_used, xg, w1, w3, w2, fc):
    n_slots, d = xg.shape
    nb = n_slots // MOE_BM
    nf = w1.shape[2] // fc
    grid_spec = pltpu.PrefetchScalarGridSpec(
        num_scalar_prefetch=2,
        grid=(nb, nf),
        in_specs=[pl.BlockSpec((MOE_BM, d), lambda b, f, be, nu: (b, 0)),
                  pl.BlockSpec((1, d, fc), lambda b, f, be, nu: (be[b], 0, f)),
                  pl.BlockSpec((1, d, fc), lambda b, f, be, nu: (be[b], 0, f)),
                  pl.BlockSpec((1, fc, d), lambda b, f, be, nu: (be[b], f, 0))],
        out_specs=pl.BlockSpec((MOE_BM, d), lambda b, f, be, nu: (b, 0)),
        scratch_shapes=[pltpu.VMEM((MOE_BM, d), F32)],
    )
    return pl.pallas_call(
        _moe_ffn_body,
        grid_spec=grid_spec,
        out_shape=jax.ShapeDtypeStruct((n_slots, d), F32),
        compiler_params=_cp(2),
        name="moe_ffn",
    )(blk_expert, n_used, xg, w1, w3, w2)


def _combine_body(y0_ref, y1_ref, gate_ref, x_ref, mod_ref, o_ref):
    d = x_ref.shape[1]
    g = gate_ref[...]
    f = g[:, 0:1] * y0_ref[...] + g[:, 1:2] * y1_ref[...]
    o_ref[...] = x_ref[...] + mod_ref[0:1, 5 * d:6 * d] * f


def _moe_combine(yk, gate_pad, x_mid, mod_l):
    n, d = x_mid.shape
    nt = n // TM
    return pl.pallas_call(
        _combine_body,
        grid=(nt,),
        in_specs=[pl.BlockSpec((TM, d), lambda i: (i, 0)), pl.BlockSpec((TM, d), lambda i: (i + nt, 0)),
                  pl.BlockSpec((TM, LANE), lambda i: (i, 0)), pl.BlockSpec((TM, d), lambda i: (i, 0)),
                  _full(mod_l.shape)],
        out_specs=pl.BlockSpec((TM, d), lambda i: (i, 0)),
        out_shape=jax.ShapeDtypeStruct((n, d), F32),
        compiler_params=_cp(1),
        name="moe_combine",
    )(yk, yk, gate_pad, x_mid, mod_l)


def _moe(h2, x_mid, mod_l, router_w, w1, w3, w2, fc):
    n, d = h2.shape
    e_n = router_w.shape[1]
    idx_pad, gate_pad = _router(h2, jnp.zeros((d, LANE), BF16).at[:, :e_n].set(router_w.astype(BF16)))
    flat_e = idx_pad[:, :TOP_K].T.reshape(-1)
    onehot = (flat_e[:, None] == jnp.arange(e_n)[None, :]).astype(jnp.int32)
    rank = jnp.take_along_axis(jnp.cumsum(onehot, axis=0) - onehot, flat_e[:, None], axis=1)[:, 0]
    counts = jnp.sum(onehot, axis=0)
    padded = (counts + MOE_BM - 1) // MOE_BM * MOE_BM
    pad_ends = jnp.cumsum(padded)
    slot = ((pad_ends - padded)[flat_e] + rank).astype(jnp.int32)
    nb = TOP_K * n // MOE_BM + e_n
    n_slots = nb * MOE_BM
    tok_of_slot = jnp.zeros((n_slots,), jnp.int32).at[slot].set(jnp.tile(jnp.arange(n, dtype=jnp.int32), TOP_K))
    blk_expert = jnp.minimum(jnp.searchsorted(pad_ends, jnp.arange(nb, dtype=jnp.int32) * MOE_BM, side="right"),
                             e_n - 1).astype(jnp.int32)
    n_used = (pad_ends[-1:] // MOE_BM).astype(jnp.int32)
    xg = _gather_rows(tok_of_slot.reshape(nb, MOE_BM), h2, MOE_BM)
    yg = _moe_ffn(blk_expert, n_used, xg, w1, w3, w2, fc)
    yk = _gather_rows(slot.reshape(-1, MOE_BM), yg, MOE_BM)
    return _moe_combine(yk, gate_pad, x_mid, mod_l)


def _rope_tables(n_ctx, n_lat):
    pos = jnp.arange(n_lat)
    half = MLA_ROPE // 4
    freqs = jnp.exp(-math.log(ROPE_BASE) * jnp.arange(half, dtype=F32) / half)
    ang_r = (pos // GRID_W).astype(F32)[:, None] * freqs[None, :]
    ang_c = (pos % GRID_W).astype(F32)[:, None] * freqs[None, :]
    one = jnp.ones((n_lat, MLA_NOPE), F32)
    zero = jnp.zeros((n_lat, MLA_NOPE), F32)
    tail1 = jnp.ones((n_lat, HEAD_PAD - MLA_QK), F32)
    tail0 = jnp.zeros((n_lat, HEAD_PAD - MLA_QK), F32)
    z8 = jnp.zeros((n_lat, half), F32)
    cr, sr, cc, sc = jnp.cos(ang_r), jnp.sin(ang_r), jnp.cos(ang_c), jnp.sin(ang_c)
    cos = jnp.concatenate([one, cr, cr, cc, cc, tail1], axis=1)
    sin_a = jnp.concatenate([zero, -sr, z8, -sc, z8, tail0], axis=1)
    sin_b = jnp.concatenate([zero, z8, sr, z8, sc, tail0], axis=1)
    ctx_pad = lambda a, v: jnp.concatenate([jnp.full((n_ctx, HEAD_PAD), v, F32), a], axis=0)
    return ctx_pad(cos, 1.0), ctx_pad(sin_a, 0.0), ctx_pad(sin_b, 0.0)


def _seg_matrix(width, seg):
    i = np.arange(width)
    return jnp.asarray((i[:, None] // seg == i[None, :] // seg).astype(np.float32), dtype=BF16)


def _block_diag2(w):
    _, r, c = w.shape
    z = jnp.zeros((r, c), w.dtype)
    return jnp.concatenate([jnp.concatenate([w[0], z], axis=1), jnp.concatenate([z, w[1]], axis=1)], axis=0)


def _pad_heads(w, n_heads, width, pad_to):
    rows = w.shape[0]
    w = w.reshape(rows, n_heads, width)
    return jnp.pad(w, ((0, 0), (0, 0), (0, pad_to - width))).reshape(rows, n_heads * pad_to)


def kernel(x, c, ctx, c_ctx, mod_w, mod_b, norm1_g, norm2_g, w_in, mla_cq_g, mla_wuq, mla_ckv_g, mla_wukv, mla_qn_g, mla_kn_g, mla_wo, na_qn_g, na_kn_g, na_rpb, na_wo, rw_mu, rw_w0, rw_w2, rw_a0, rw_a2, rw_g2, rw_kk, rw_ka, rw_rk, rw_ln_g, rw_ln_b, rw_wo, w_out, ffn_w1, ffn_w3, ffn_w2, moe_router, moe_w1, moe_w3, moe_w2):
    b, n_lat, d = x.shape
    n_ctx = ctx.shape[1]
    depth = mod_w.shape[0]
    assert b == 1 and n_ctx % TM == 0 and n_lat % TM == 0 and n_lat % (NA_ROWS * GRID_W) == 0
    assert n_lat // GRID_W >= NA_KROWS and n_ctx % RW_CHUNK == 0

    mod = _modulation(c, c_ctx, mod_w, mod_b)
    x_all = jnp.concatenate([ctx[0], x[0]], axis=0)
    rope_tabs = _rope_tables(n_ctx, n_lat)
    seg64 = _seg_matrix(RW_WIDTH, RW_HEAD_DIM)

    o_ckv = MLA_Q_LORA
    o_kr = o_ckv + MLA_KV_LORA
    o_na = o_kr + MLA_ROPE
    o_rz = o_na + 3 * NA_WIDTH
    o_gate = o_rz + RW_IN

    for l in range(depth):
        need_ctx = l < depth - 1
        mod_l = mod[l]
        g1 = norm1_g[l][None, :]
        g2n = norm2_g[l][None, :]
        wi = w_in[l]

        w_a = jnp.pad(wi[:, :o_na], ((0, 0), (0, 6 * LANE - o_na))).astype(BF16)
        w_na = wi[:, o_na:o_rz].astype(BF16)
        w_rz = wi[:, o_rz:o_gate].astype(BF16)
        w_g = wi[:, o_gate:].astype(BF16)
        wuq = _pad_heads(mla_wuq[l], MLA_HEADS, MLA_QK, HEAD_PAD).astype(BF16)
        wukv = mla_wukv[l].reshape(MLA_KV_LORA, MLA_HEADS, MLA_NOPE + MLA_V)
        wuk = jnp.pad(wukv[:, :, :MLA_NOPE], ((0, 0), (0, 0), (0, HEAD_PAD - MLA_NOPE))).reshape(MLA_KV_LORA, -1).astype(BF16)
        wuvt = wukv[:, :, MLA_NOPE:].reshape(MLA_KV_LORA, -1).T.astype(BF16)
        qng = jnp.pad(mla_qn_g[l], (0, HEAD_PAD - MLA_QK))[None, :]
        kng = jnp.pad(mla_kn_g[l], (0, HEAD_PAD - MLA_QK))[None, :]
        mla_wts = (w_a, mla_cq_g[l][None, :], wuq, mla_ckv_g[l][None, :], wuk, wuvt, qng, kng)

        qa, ka, vta = _mla_proj(x_all, mod_l, g1, mla_wts, rope_tabs, n_ctx)
        yat = _mla_attn(qa, ka, vta, n_ctx, need_ctx)

        qb, kb, vb = _na_proj(x_all, mod_l, g1, w_na, jnp.tile(na_qn_g[l], NA_HEADS)[None, :],
                              jnp.tile(na_kn_g[l], NA_HEADS)[None, :], seg64, n_ctx)
        yb = _na_attn(qb, kb, vb, _na_bias_tables(na_rpb[l], n_lat // GRID_W), n_ctx)
        if need_ctx:
            yb = jnp.concatenate([_ctx_attn(qb, kb, vb, n_ctx), yb], axis=0)

        z = _rz_proj(x_all, mod_l, g1, w_rz, n_ctx)
        r, v, kkn, lw, beta, kd, gd = _rw_prep(
            z, rw_mu[l], rw_w0[l].reshape(1, -1), _block_diag2(rw_w2[l]).astype(BF16), rw_a0[l].reshape(1, -1),
            _block_diag2(rw_a2[l]).astype(BF16), rw_kk[l][None, :], rw_ka[l][None, :], seg64, n_ctx)
        yf, ybw = _rw_chunks(r, v, kkn, lw, beta, kd, n_ctx)

        x_mid, h2 = _merge(x_all, mod_l, g1, g2n, w_g, yat, yb, yf, ybw, r, v, kd, gd,
                           rw_g2[l].astype(BF16), rw_rk[l].reshape(1, -1), rw_ln_g[l][None, :], rw_ln_b[l][None, :], seg64,
                           mla_wo[l].astype(BF16), na_wo[l].astype(BF16), rw_wo[l].astype(BF16), w_out[l].astype(BF16),
                           n_ctx, need_ctx, BF16 if l % 2 == 0 else F32)

        if l % 2 == 0:
            fc = FFN_FC
            w1 = ffn_w1[l // 2]
            nf = w1.shape[1] // fc
            w1r = w1.reshape(d, nf, fc).transpose(1, 0, 2).astype(BF16)
            w3r = ffn_w3[l // 2].reshape(d, nf, fc).transpose(1, 0, 2).astype(BF16)
            w2r = ffn_w2[l // 2].reshape(nf, fc, d).astype(BF16)
            if need_ctx:
                x_all = _ffn(h2, x_mid, mod_l, w1r, w3r, w2r, n_ctx)
            else:
                x_lat = _ffn(h2, x_mid, mod_l, w1r, w3r, w2r, 0)
        else:
            moe_args = (moe_router[l // 2], moe_w1[l // 2].astype(BF16), moe_w3[l // 2].astype(BF16),
                        moe_w2[l // 2].astype(BF16), MOE_FC)
            if need_ctx:
                lat = _moe(h2[n_ctx:], x_mid[n_ctx:], mod_l, *moe_args)
                ctx_rows = _moe(h2[:n_ctx], x_mid[:n_ctx], mod_l.at[0].set(mod_l[1]), *moe_args)
                x_all = jnp.concatenate([ctx_rows, lat], axis=0)
            else:
                x_lat = _moe(h2, x_mid, mod_l, *moe_args)
    return x_lat[None]
```

```python
import functools
import math

import numpy as np
import jax
import jax.numpy as jnp
from jax import lax
from jax.experimental import pallas as pl
from jax.experimental.pallas import tpu as pltpu

F32 = jnp.float32
BF16 = jnp.bfloat16

GRID_W = 64
MLA_HEADS, MLA_NOPE, MLA_ROPE, MLA_V = 8, 64, 32, 64
MLA_QK = MLA_NOPE + MLA_ROPE
MLA_Q_LORA, MLA_KV_LORA = 384, 256
NA_HEADS, NA_HEAD_DIM = 8, 64
NA_WIDTH = NA_HEADS * NA_HEAD_DIM
NA_WIN_ROWS, NA_WIN_COLS = 8, 16
RW_HEADS, RW_HEAD_DIM = 8, 64
RW_WIDTH = RW_HEADS * RW_HEAD_DIM
RW_DECAY_LORA, RW_ICLR_LORA, RW_GATE_LORA = 64, 64, 128
RW_GN_EPS = 64e-5
RW_IN = 3 * RW_WIDTH + 2 * RW_DECAY_LORA + 2 * RW_ICLR_LORA + RW_GATE_LORA
N_EXPERTS, TOP_K = 8, 2
ROPE_BASE = 10000.0
NORM_EPS = 1e-6

LANE = 128
SUBLANE = 8
VMEM_LIMIT = 56 * 1024 * 1024

TM = 256
HEAD_PAD = LANE
MLA_V_ROWS = 80
MLA_Q_STREAMS = 2
MLA_FAST_UNROLL = 65
MLA_KEY_TILES_PER_STEP = 5
MLA_FAST_BOUND = 40.0
RW_CHUNK = 64
RW_GROUP = 4
NA_ROWS = 4
NA_KROWS = NA_ROWS + NA_WIN_ROWS
MOE_BM = 512
MOE_FC = 512
FFN_FC = 256
NEG_BIG = -1e30


def _cp(n_axes, vmem=VMEM_LIMIT):
    return pltpu.CompilerParams(dimension_semantics=("arbitrary",) * n_axes, vmem_limit_bytes=vmem)


def _dot(a, b):
    return jnp.dot(a, b, preferred_element_type=F32)


def _dot_nt(a, b):
    return lax.dot_general(a, b, (((1,), (1,)), ((), ())), preferred_element_type=F32)


def _dot_tn(a, b):
    return lax.dot_general(a, b, (((0,), (0,)), ((), ())), preferred_element_type=F32)


def _split2(x):
    hi = x.astype(BF16)
    lo = (x - hi.astype(F32)).astype(BF16)
    return hi, lo


def _split3(x):
    hi = x.astype(BF16)
    r1 = x - hi.astype(F32)
    mid = r1.astype(BF16)
    lo = (r1 - mid.astype(F32)).astype(BF16)
    return hi, mid, lo


def _seg_sum(x, seg01):
    hi, lo = _split2(x)
    return _dot(hi, seg01) + _dot(lo, seg01)


def _full(shape):
    nd = len(shape)
    return pl.BlockSpec(shape, lambda *_: (0,) * nd)


def _modulated_norm(x, g, mod, first_row, n_ctx, which):
    d = x.shape[1]
    off = 0 if which == 1 else 3 * d
    y = x * lax.rsqrt(jnp.mean(x * x, axis=-1, keepdims=True) + NORM_EPS) * g
    rows = first_row + lax.broadcasted_iota(jnp.int32, (x.shape[0], 1), 0)
    is_ctx = rows < n_ctx
    sh = jnp.where(is_ctx, mod[1:2, off:off + d], mod[0:1, off:off + d])
    sc = jnp.where(is_ctx, mod[1:2, off + d:off + 2 * d], mod[0:1, off + d:off + 2 * d])
    return y * (1.0 + sc) + sh


def _gate_rows(mod, first_row, n_rows, n_ctx, which):
    d = mod.shape[1] // 6
    off = 2 * d if which == 1 else 5 * d
    rows = first_row + lax.broadcasted_iota(jnp.int32, (n_rows, 1), 0)
    return jnp.where(rows < n_ctx, mod[1:2, off:off + d], mod[0:1, off:off + d])


def _mod_body(cc_ref, w_ref, b_ref, o_ref):
    cc = cc_ref[...]
    s = cc * jax.nn.sigmoid(cc)
    o_ref[0] = _dot(s.astype(BF16), w_ref[0].astype(BF16)) + b_ref[0]


def _modulation(c, c_ctx, mod_w, mod_b):
    depth, d, d6 = mod_w.shape
    nt = d6 // 4
    cc = jnp.zeros((SUBLANE, d), F32).at[0].set(c[0]).at[1].set(c_ctx)
    return pl.pallas_call(
        _mod_body,
        grid=(depth, d6 // nt),
        in_specs=[_full((SUBLANE, d)),
                  pl.BlockSpec((1, d, nt), lambda l, j: (l, 0, j)),
                  pl.BlockSpec((1, 1, nt), lambda l, j: (l, 0, j))],
        out_specs=pl.BlockSpec((1, SUBLANE, nt), lambda l, j: (l, 0, j)),
        out_shape=jax.ShapeDtypeStruct((depth, SUBLANE, d6), F32),
        compiler_params=_cp(2),
        name="modulation",
    )(cc, mod_w, mod_b.reshape(depth, 1, d6))


def _rms(x, width):
    return x * lax.rsqrt(jnp.sum(x * x, axis=-1, keepdims=True) * (1.0 / width) + NORM_EPS)


def _rope(x, cos, sin_a, sin_b):
    return x * cos + pltpu.roll(x, HEAD_PAD - 8, 1) * sin_a + pltpu.roll(x, 8, 1) * sin_b


def _mla_proj_body(n_ctx, x_ref, mod_ref, g1_ref, wa_ref, cqg_ref, wuq_ref, ckvg_ref, wuk_ref, wuvt_ref,
                   qng_ref, kng_ref, shift_ref, cos_ref, sa_ref, sb_ref, q_ref, k_ref, vt_ref):
    tm = x_ref.shape[0]
    h = _modulated_norm(x_ref[...], g1_ref[...], mod_ref[...], pl.program_id(0) * tm, n_ctx, 1).astype(BF16)
    z = _dot(h, wa_ref[...])
    cq = z[:, :MLA_Q_LORA]
    ckv = z[:, MLA_Q_LORA:MLA_Q_LORA + MLA_KV_LORA]
    kr = z[:, MLA_Q_LORA + MLA_KV_LORA:]
    cqn = (_rms(cq, MLA_Q_LORA) * cqg_ref[...]).astype(BF16)
    ckvn = (_rms(ckv, MLA_KV_LORA) * ckvg_ref[...]).astype(BF16)
    q = _dot(cqn, wuq_ref[...])
    kn = _dot(ckvn, wuk_ref[...])
    vt = _dot_nt(wuvt_ref[...], ckvn)
    kr_sh = pltpu.roll(kr, MLA_NOPE, 1)
    cos, sa, sb = cos_ref[...], sa_ref[...], sb_ref[...]
    q_scale = MLA_QK ** -0.5 * math.log2(math.e)
    one_lane = jnp.where(lax.broadcasted_iota(jnp.int32, (1, HEAD_PAD), 1) == MLA_QK, 1.0, 0.0)
    pad_rows = MLA_V_ROWS - MLA_V
    ones_rows = jnp.where(lax.broadcasted_iota(jnp.int32, (pad_rows, tm), 0) == 0, 1.0, 0.0).astype(BF16)
    for hh in range(MLA_HEADS):
        sl = slice(hh * HEAD_PAD, (hh + 1) * HEAD_PAD)
        qh = _rope(_rms(q[:, sl], MLA_QK) * qng_ref[...], cos, sa, sb) * q_scale + shift_ref[...]
        kh = _rope(_rms(kn[:, sl] + kr_sh, MLA_QK) * kng_ref[...], cos, sa, sb) + one_lane
        q_ref[:, sl] = qh.astype(BF16)
        k_ref[:, sl] = kh.astype(BF16)
        vt_ref[0, hh * MLA_V_ROWS:hh * MLA_V_ROWS + MLA_V, :] = vt[hh * MLA_V:(hh + 1) * MLA_V].astype(BF16)
        vt_ref[0, hh * MLA_V_ROWS + MLA_V:(hh + 1) * MLA_V_ROWS, :] = ones_rows


def _mla_proj(x_all, mod_l, g1, wts, rope_tabs, n_ctx):
    t, d = x_all.shape
    nt = t // TM
    hp = MLA_HEADS * HEAD_PAD
    tok = lambda w: pl.BlockSpec((TM, w), lambda i: (i, 0))
    return pl.pallas_call(
        functools.partial(_mla_proj_body, n_ctx),
        grid=(nt,),
        in_specs=[tok(d), _full(mod_l.shape), _full(g1.shape)] + [_full(w.shape) for w in wts]
                 + [tok(HEAD_PAD), tok(HEAD_PAD), tok(HEAD_PAD)],
        out_specs=[tok(hp), tok(hp), pl.BlockSpec((1, MLA_HEADS * MLA_V_ROWS, TM), lambda i: (i, 0, 0))],
        out_shape=[jax.ShapeDtypeStruct((t, hp), BF16), jax.ShapeDtypeStruct((t, hp), BF16),
                   jax.ShapeDtypeStruct((nt, MLA_HEADS * MLA_V_ROWS, TM), BF16)],
        compiler_params=_cp(1),
        name="mla_proj",
    )(x_all, mod_l, g1, *wts, *rope_tabs)


def _mla_attn_fast_body(unroll, n_streams, *refs):
    q_refs, (k_ref, vt_ref, o_ref) = refs[:n_streams], refs[n_streams:]
    tq = q_refs[0].shape[0]
    nk, rows_v, tk = vt_ref.shape
    qs = [q_ref[...] for q_ref in q_refs]

    def scores(j):
        j = jnp.minimum(j, nk - 1)
        kj = k_ref[pl.ds(pl.multiple_of(j * tk, tk), tk), :]
        return tuple(_dot_nt(kj, q) for q in qs)

    def probs(ss):
        return tuple(jnp.exp2(s).astype(BF16) for s in ss)

    def step(it, carry):
        accs, ss, ps = carry
        for u in range(unroll):
            j = it * unroll + u
            s_new = scores(j + 2)
            p_new = probs(ss)
            vj = vt_ref[j]
            accs = tuple(a + _dot(vj, p) for a, p in zip(accs, ps))
            ss, ps = s_new, p_new
        return accs, ss, ps

    init = (tuple(jnp.zeros((rows_v, tq), F32) for _ in range(n_streams)), scores(1), probs(scores(0)))
    accs, _, _ = lax.fori_loop(0, nk // unroll, step, init)
    for i in range(n_streams):
        o_ref[:, i * tq:(i + 1) * tq] = accs[i][:MLA_V] / accs[i][MLA_V:MLA_V + 1]


def _mla_attn_online_body(kb, q_ref, k_ref, vt_ref, o_ref):
    tq = q_ref.shape[0]
    nk, _, tk = vt_ref.shape
    nblk = nk // kb
    rows = kb * tk
    q = q_ref[...]

    def scores(j):
        return _dot_nt(k_ref[pl.ds(pl.multiple_of(j * rows, rows), rows), :], q)

    def consume(j, m, l, acc, s):
        m_new = jnp.maximum(m, jnp.max(s, axis=0, keepdims=True))
        alpha = jnp.exp2(m - m_new)
        p = jnp.exp2(s - m_new)
        l = alpha * l + jnp.sum(p, axis=0, keepdims=True)
        pb = p.astype(BF16)
        pv = _dot(vt_ref[j * kb][:MLA_V], pb[0:tk])
        for i in range(1, kb):
            pv = pv + _dot(vt_ref[j * kb + i][:MLA_V], pb[i * tk:(i + 1) * tk])
        return m_new, l, alpha * acc + pv

    def step(j, carry):
        m, l, acc, s = carry
        s_next = scores(j + 1)
        m, l, acc = consume(j, m, l, acc, s)
        return m, l, acc, s_next

    init = (jnp.full((1, tq), NEG_BIG, F32), jnp.zeros((1, tq), F32), jnp.zeros((MLA_V, tq), F32), scores(0))
    m, l, acc, s = lax.fori_loop(0, nblk - 1, step, init)
    _, l, acc = consume(nblk - 1, m, l, acc, s)
    o_ref[...] = acc / l


def _largest_divisor(n, cap):
    return max(b for b in range(1, cap + 1) if n % b == 0)


def _mla_attn(q, k, vt, q_tile0, nq, nk, fast):
    if fast:
        ns = MLA_Q_STREAMS if nq % MLA_Q_STREAMS == 0 else 1
        body = functools.partial(_mla_attn_fast_body, _largest_divisor(nk, MLA_FAST_UNROLL), ns)
    else:
        ns = 1
        body = functools.partial(_mla_attn_online_body, _largest_divisor(nk, MLA_KEY_TILES_PER_STEP))
    q_specs = [pl.BlockSpec((TM, HEAD_PAD), lambda h, i, s=s: (ns * i + q_tile0 + s, h)) for s in range(ns)]
    return pl.pallas_call(
        body,
        grid=(MLA_HEADS, nq // ns),
        in_specs=q_specs + [pl.BlockSpec((nk * TM, HEAD_PAD), lambda h, i: (0, h)),
                            pl.BlockSpec((nk, MLA_V_ROWS, TM), lambda h, i: (0, h, 0))],
        out_specs=pl.BlockSpec((MLA_V, ns * TM), lambda h, i: (h, i)),
        out_shape=jax.ShapeDtypeStruct((MLA_HEADS * MLA_V, nq * TM), F32),
        compiler_params=_cp(2),
        name="mla_attn_fast" if fast else "mla_attn_online",
    )(*([q] * ns), k, vt)


def _na_proj_body(n_ctx, x_ref, mod_ref, g1_ref, w_ref, qg_ref, kg_ref, seg_ref, q_ref, k_ref, v_ref):
    tm = x_ref.shape[0]
    h = _modulated_norm(x_ref[...], g1_ref[...], mod_ref[...], pl.program_id(0) * tm, n_ctx, 1).astype(BF16)
    z = _dot(h, w_ref[...])
    seg = seg_ref[...]
    inv = 1.0 / NA_HEAD_DIM

    def head_norm(y, g):
        return y * lax.rsqrt(_seg_sum(y * y, seg) * inv + NORM_EPS) * g

    q_ref[...] = (head_norm(z[:, :NA_WIDTH], qg_ref[...]) * (NA_HEAD_DIM ** -0.5)).astype(BF16)
    k_ref[...] = head_norm(z[:, NA_WIDTH:2 * NA_WIDTH], kg_ref[...]).astype(BF16)
    v_ref[...] = z[:, 2 * NA_WIDTH:].astype(BF16)


def _na_proj(x_all, mod_l, g1, w_na, qg, kg, seg, n_ctx):
    t, d = x_all.shape
    tok = lambda w: pl.BlockSpec((TM, w), lambda i: (i, 0))
    return pl.pallas_call(
        functools.partial(_na_proj_body, n_ctx),
        grid=(t // TM,),
        in_specs=[tok(d), _full(mod_l.shape), _full(g1.shape), _full(w_na.shape), _full(qg.shape), _full(kg.shape),
                  _full(seg.shape)],
        out_specs=[tok(NA_WIDTH)] * 3,
        out_shape=[jax.ShapeDtypeStruct((t, NA_WIDTH), BF16)] * 3,
        compiler_params=_cp(1),
        name="na_proj",
    )(x_all, mod_l, g1, w_na, qg, kg, seg)


def _head_masks():
    lane = lax.broadcasted_iota(jnp.int32, (1, LANE), 1)
    return [(lane // NA_HEAD_DIM == h) for h in range(LANE // NA_HEAD_DIM)]


def _na_attn_body(n_ctx, n_grid_rows, q_ref, k_ref, v_ref, bias_ref, o_ref):
    nq = q_ref.shape[0]
    nkl = NA_KROWS * GRID_W
    g = pl.program_id(1)
    kr0 = jnp.clip(g * NA_ROWS - NA_WIN_ROWS // 2, 0, n_grid_rows - NA_KROWS)
    start = pl.multiple_of(n_ctx + kr0 * GRID_W, GRID_W)
    k_loc = k_ref[pl.ds(start, nkl), :]
    v_loc = v_ref[pl.ds(start, nkl), :]
    k_ctx = k_ref[0:n_ctx, :]
    v_ctx = v_ref[0:n_ctx, :]
    q = q_ref[...]
    out = jnp.zeros((nq, LANE), F32)
    for h, hm in enumerate(_head_masks()):
        qh = jnp.where(hm, q, jnp.zeros_like(q))
        s_loc = _dot_nt(qh, k_loc) + bias_ref[0, h]
        s_ctx = _dot_nt(qh, k_ctx)
        m = jnp.maximum(jnp.max(s_loc, axis=-1, keepdims=True), jnp.max(s_ctx, axis=-1, keepdims=True))
        p_loc = jnp.exp(s_loc - m)
        p_ctx = jnp.exp(s_ctx - m)
        l = jnp.sum(p_loc, axis=-1, keepdims=True) + jnp.sum(p_ctx, axis=-1, keepdims=True)
        o = (_dot(p_loc.astype(BF16), v_loc) + _dot(p_ctx.astype(BF16), v_ctx)) / l
        out = jnp.where(hm, o, out)
    o_ref[...] = out


def _na_attn(q, k, v, bias, n_ctx):
    t = q.shape[0]
    n_lat = t - n_ctx
    n_grid_rows = n_lat // GRID_W
    nq = NA_ROWS * GRID_W
    ng = n_grid_rows // NA_ROWS
    q_blk0 = n_ctx // nq
    npairs = NA_WIDTH // LANE

    def case(g):
        return jnp.where(g == 0, 0, jnp.where(g == ng - 1, 2, 1))

    return pl.pallas_call(
        functools.partial(_na_attn_body, n_ctx, n_grid_rows),
        grid=(npairs, ng),
        in_specs=[pl.BlockSpec((nq, LANE), lambda p, g: (g + q_blk0, p)),
                  pl.BlockSpec((t, LANE), lambda p, g: (0, p)),
                  pl.BlockSpec((t, LANE), lambda p, g: (0, p)),
                  pl.BlockSpec((1, LANE // NA_HEAD_DIM, nq, NA_KROWS * GRID_W), lambda p, g: (case(g), p, 0, 0))],
        out_specs=pl.BlockSpec((nq, LANE), lambda p, g: (g, p)),
        out_shape=jax.ShapeDtypeStruct((n_lat, NA_WIDTH), F32),
        compiler_params=_cp(2),
        name="na_attn",
    )(q, k, v, bias)


def _ctx_attn_body(q_ref, k_ref, v_ref, o_ref):
    q, k, v = q_ref[...], k_ref[...], v_ref[...]
    out = jnp.zeros(o_ref.shape, F32)
    for hm in _head_masks():
        s = _dot_nt(jnp.where(hm, q, jnp.zeros_like(q)), k)
        p = jnp.exp(s - jnp.max(s, axis=-1, keepdims=True))
        o = _dot(p.astype(BF16), v) / jnp.sum(p, axis=-1, keepdims=True)
        out = jnp.where(hm, o, out)
    o_ref[...] = out


def _ctx_attn(q, k, v, n_ctx):
    npairs = NA_WIDTH // LANE
    blk = pl.BlockSpec((n_ctx, LANE), lambda p: (0, p))
    return pl.pallas_call(
        _ctx_attn_body,
        grid=(npairs,),
        in_specs=[blk, blk, blk],
        out_specs=blk,
        out_shape=jax.ShapeDtypeStruct((n_ctx, NA_WIDTH), F32),
        compiler_params=_cp(1),
        name="ctx_attn",
    )(q, k, v)


def _na_bias_tables(rpb):
    h = rpb.shape[0]
    qc = np.arange(GRID_W)[:, None]
    kc = np.arange(GRID_W)[None, :]
    col_start = np.clip(qc - NA_WIN_COLS // 2, 0, GRID_W - NA_WIN_COLS)
    col_ok = (kc >= col_start) & (kc < col_start + NA_WIN_COLS)
    sel = (kc - qc + NA_WIN_COLS - 1)[None] == np.arange(2 * NA_WIN_COLS - 1)[:, None, None]
    by_row = jnp.einsum("hrd,dqk->hrqk", rpb.astype(F32), jnp.asarray(sel & col_ok[None], F32),
                        precision=lax.Precision.HIGHEST)
    by_row = jnp.where(col_ok[None, None], by_row, NEG_BIG)
    masked = jnp.full((h, GRID_W, GRID_W), NEG_BIG, F32)
    tabs = []
    for off, first_row in ((0, lambda qr: 0), (NA_WIN_ROWS // 2, lambda qr: qr), (NA_WIN_ROWS, lambda qr: NA_ROWS)):
        q_rows = []
        for qr in range(NA_ROWS):
            lo = first_row(qr)
            blocks = [by_row[:, kr - off - qr + NA_WIN_ROWS - 1] if lo <= kr < lo + NA_WIN_ROWS else masked
                      for kr in range(NA_KROWS)]
            q_rows.append(jnp.concatenate(blocks, axis=2))
        tabs.append(jnp.concatenate(q_rows, axis=1))
    return jnp.stack(tabs)


def _rz_proj_body(n_ctx, x_ref, mod_ref, g1_ref, w_ref, z_ref):
    tm = x_ref.shape[0]
    h = _modulated_norm(x_ref[...], g1_ref[...], mod_ref[...], pl.program_id(0) * tm, n_ctx, 1).astype(BF16)
    z_ref[...] = _dot(h, w_ref[...])


def _rz_proj(x_all, mod_l, g1, w_rz, n_ctx):
    t, d = x_all.shape
    tok = lambda w: pl.BlockSpec((TM, w), lambda i: (i, 0))
    return pl.pallas_call(
        functools.partial(_rz_proj_body, n_ctx),
        grid=(t // TM,),
        in_specs=[tok(d), _full(mod_l.shape), _full(g1.shape), _full(w_rz.shape)],
        out_specs=tok(RW_IN),
        out_shape=jax.ShapeDtypeStruct((t, RW_IN), F32),
        compiler_params=_cp(1),
        name="rz_proj",
    )(x_all, mod_l, g1, w_rz)


def _rw_prep_body(n_ctx, n_tok, z_ref, zp_ref, zn_ref, mu_ref, w0_ref, w2_ref, a0_ref, a2_ref, kk_ref, ka_ref, seg_ref,
                  r_ref, v_ref, kkn_ref, lw_ref, beta_ref, kd_ref, gd_ref):
    tm = z_ref.shape[0]
    z = z_ref[...]
    loc = lax.broadcasted_iota(jnp.int32, (tm, 1), 0)
    rows = pl.program_id(0) * tm + loc
    zp = jnp.where(loc == 0, zp_ref[0][SUBLANE - 1:SUBLANE, :], pltpu.roll(z, 1, 0))
    zp = jnp.where((rows == 0) | (rows == n_ctx), 0.0, zp)
    zn = jnp.where(loc == tm - 1, zn_ref[0][0:1, :], pltpu.roll(z, tm - 1, 0))
    zn = jnp.where((rows == n_ctx - 1) | (rows == n_tok - 1), 0.0, zn)
    mu = mu_ref[...]
    zs = z + mu[0:1] * (zp - z) + mu[1:2] * (zn - z)
    w = RW_WIDTH
    r, k, v = zs[:, :w], zs[:, w:2 * w], zs[:, 2 * w:3 * w]
    wd = zs[:, 3 * w:3 * w + 2 * RW_DECAY_LORA]
    ad = zs[:, 3 * w + 2 * RW_DECAY_LORA:3 * w + 2 * RW_DECAY_LORA + 2 * RW_ICLR_LORA]
    gd_ref[...] = zs[:, RW_IN - RW_GATE_LORA:]
    u = w0_ref[...] + _dot(jnp.tanh(wd).astype(BF16), w2_ref[...])
    a = jax.nn.sigmoid(a0_ref[...] + _dot(ad.astype(BF16), a2_ref[...]))
    lw = -jax.nn.sigmoid(u) * math.exp(-0.5)
    kkr = k * kk_ref[...]
    kkn = kkr / jnp.maximum(jnp.sqrt(_seg_sum(kkr * kkr, seg_ref[...])), 1e-12)
    r_ref[...] = r
    v_ref[...] = v
    kkn_ref[...] = kkn
    for d in range(2):
        a_d = a[:, d * w:(d + 1) * w]
        lw_ref[d] = lw[:, d * w:(d + 1) * w]
        beta_ref[d] = kkn * a_d
        kd_ref[d] = k * (1.0 + (a_d - 1.0) * ka_ref[...])


def _rw_prep(z, mu, w0, w2b, a0, a2b, k_k, k_a, seg, n_ctx):
    t = z.shape[0]
    n8 = TM // SUBLANE
    z3 = z.reshape(t // SUBLANE, SUBLANE, RW_IN)
    tok = lambda w: pl.BlockSpec((TM, w), lambda i: (i, 0))
    tok2 = pl.BlockSpec((2, TM, RW_WIDTH), lambda i: (0, i, 0))
    f = jax.ShapeDtypeStruct((t, RW_WIDTH), F32)
    f2 = jax.ShapeDtypeStruct((2, t, RW_WIDTH), F32)
    return pl.pallas_call(
        functools.partial(_rw_prep_body, n_ctx, t),
        grid=(t // TM,),
        in_specs=[tok(RW_IN),
                  pl.BlockSpec((1, SUBLANE, RW_IN), lambda i: (jnp.maximum(i * n8 - 1, 0), 0, 0)),
                  pl.BlockSpec((1, SUBLANE, RW_IN), lambda i: (jnp.minimum((i + 1) * n8, t // SUBLANE - 1), 0, 0)),
                  _full(mu.shape), _full(w0.shape), _full(w2b.shape), _full(a0.shape), _full(a2b.shape),
                  _full(k_k.shape), _full(k_a.shape), _full(seg.shape)],
        out_specs=[tok(RW_WIDTH), tok(RW_WIDTH), tok(RW_WIDTH), tok2, tok2, tok2, tok(RW_GATE_LORA)],
        out_shape=[f, f, f, f2, f2, f2, jax.ShapeDtypeStruct((t, RW_GATE_LORA), F32)],
        compiler_params=_cp(1),
        name="rw_prep",
    )(z, z3, z3, mu, w0, w2b, a0, a2b, k_k, k_a, seg)


def _rw_group_steps(probs):
    c, gl = probs[0][0].shape
    n = gl // RW_GROUP
    revs = [p[7] for p in probs]
    t_i = lax.broadcasted_iota(jnp.int32, (c, c), 0)
    s_i = lax.broadcasted_iota(jnp.int32, (c, c), 1)
    tri = {False: (s_i <= t_i).astype(BF16), True: (s_i >= t_i).astype(BF16)}
    diag_blk = (lax.broadcasted_iota(jnp.int32, (RW_GROUP * c, gl), 0) // c
                == lax.broadcasted_iota(jnp.int32, (RW_GROUP * c, gl), 1) // n)
    tt = lax.broadcasted_iota(jnp.int32, (c, RW_GROUP * c), 0)
    ss = lax.broadcasted_iota(jnp.int32, (c, RW_GROUP * c), 1) % c
    strict = {False: ss < tt, True: ss > tt}
    incl4 = {False: ss <= tt, True: ss >= tt}
    eye4 = jnp.where(ss == tt, 1.0, 0.0)

    def x4(m):
        return jnp.where(diag_blk, jnp.concatenate([m] * RW_GROUP, axis=0), jnp.zeros((), m.dtype))

    def each(fn, *lists):
        return [fn(*args) for args in zip(*lists)]

    lws = [p[3] for p in probs]
    parts = [_split3(lw) for lw in lws]
    g_inc = each(lambda rev, hml: _dot(tri[rev], hml[0]) + _dot(tri[rev], hml[1]) + _dot(tri[rev], hml[2]), revs, parts)
    e_neg = [jnp.exp(-g) for g in g_inc]
    a_t = each(lambda p, g: (-p[2] * jnp.exp(g - p[3])).astype(BF16), probs, g_inc)
    b_t = each(lambda p, e: (p[4] * e).astype(BF16), probs, e_neg)
    k_t = each(lambda p, e: (p[5] * e).astype(BF16), probs, e_neg)
    r_t = each(lambda p, g: (p[0] * jnp.exp(g)).astype(BF16), probs, g_inc)
    v_bf = [p[1].astype(BF16) for p in probs]
    decay_c = [jnp.exp(jnp.sum(lw, axis=0, keepdims=True)) for lw in lws]

    xb, xk, xv, xa = [each(x4, m) for m in (b_t, k_t, v_bf, a_t)]
    l_ab = each(lambda rev, a, x: jnp.where(strict[rev], _dot_nt(a, x), 0.0), revs, a_t, xb)
    l_ak = each(lambda rev, a, x: jnp.where(strict[rev], _dot_nt(a, x), 0.0).astype(BF16), revs, a_t, xk)
    m_rb = each(lambda rev, a, x: jnp.where(incl4[rev], _dot_nt(a, x), 0.0).astype(BF16), revs, r_t, xb)
    m_rk = each(lambda rev, a, x: jnp.where(incl4[rev], _dot_nt(a, x), 0.0).astype(BF16), revs, r_t, xk)
    lv = each(lambda l, x: _dot(l, x).astype(BF16), l_ak, xv)

    pw = l_ab
    tinv = [eye4 + l for l in l_ab]
    k2 = 1
    while 2 * k2 < c:
        pw = each(lambda p: _dot(p.astype(BF16), x4(p.astype(BF16))), pw)
        tinv = each(lambda t, p: t + _dot(t.astype(BF16), x4(p.astype(BF16))), tinv, pw)
        k2 *= 2
    tinv = [t.astype(BF16) for t in tinv]

    wm = each(lambda t, x: _dot(t, x).astype(BF16), tinv, xa)
    u0 = each(lambda t, l: _dot(t, x4(l)), tinv, lv)
    s_bf = [p[6] for p in probs]
    u_bf = each(lambda w, s, u: (_dot_nt(w, s) + u).astype(BF16), wm, s_bf, u0)
    y = each(lambda r, s, mb, u, mk, x: _dot_nt(r, s) + _dot(mb, x4(u)) + _dot(mk, x), r_t, s_bf, m_rb, u_bf, m_rk, xv)
    ds = each(lambda u, b, v, k: _dot_tn(u, b) + _dot_tn(v, k), u_bf, b_t, v_bf, k_t)
    return list(zip(y, ds, decay_c))


def _rw_chunk_body(nc_ctx, rf_ref, vf_ref, kf_ref, lwf_ref, bf_ref, kdf_ref,
                   rb_ref, vb_ref, kb_ref, lwb_ref, bb_ref, kdb_ref, yf_ref, yb_ref, s_ref):
    @pl.when(pl.program_id(0) == 0)
    def _():
        s_ref[...] = jnp.zeros_like(s_ref)

    gl = RW_GROUP * RW_HEAD_DIM
    blk = (lax.broadcasted_iota(jnp.int32, (gl, gl), 0) // RW_HEAD_DIM
           == lax.broadcasted_iota(jnp.int32, (gl, gl), 1) // RW_HEAD_DIM)
    dirs = ((rf_ref, vf_ref, kf_ref, lwf_ref, bf_ref, kdf_ref, yf_ref, False),
            (rb_ref, vb_ref, kb_ref, lwb_ref, bb_ref, kdb_ref, yb_ref, True))
    probs, outs = [], []
    for d, (r_ref, v_ref, k_ref, lw_ref, b_ref, kd_ref, y_ref, rev) in enumerate(dirs):
        for g in range(RW_WIDTH // gl):
            sl = slice(g * gl, (g + 1) * gl)
            idx = d * (RW_WIDTH // gl) + g
            probs.append((r_ref[:, sl], v_ref[:, sl], k_ref[:, sl], lw_ref[0][:, sl], b_ref[0][:, sl],
                          kd_ref[0][:, sl], s_ref[idx].astype(BF16), rev))
            outs.append((y_ref, sl, idx))
    for (y_ref, sl, idx), (y, ds, dec) in zip(outs, _rw_group_steps(probs)):
        y_ref[:, sl] = y
        s_ref[idx] = (s_ref[idx] + jnp.where(blk, ds, 0.0)) * dec


def _rw_chunks(r, v, kkn, lw, beta, kd, n_ctx):
    t = r.shape[0]
    c = RW_CHUNK
    nc = t // c
    nc_ctx = n_ctx // c

    def fwd(i):
        return i

    def bwd(i):
        return jnp.where(i < nc_ctx, nc_ctx - 1 - i, nc - 1 - (i - nc_ctx))

    def specs(m):
        one = pl.BlockSpec((c, RW_WIDTH), lambda i: (m(i), 0))
        return one

    def specs2(m, d):
        return pl.BlockSpec((1, c, RW_WIDTH), lambda i: (d, m(i), 0))

    in_specs = [specs(fwd), specs(fwd), specs(fwd), specs2(fwd, 0), specs2(fwd, 0), specs2(fwd, 0),
                specs(bwd), specs(bwd), specs(bwd), specs2(bwd, 1), specs2(bwd, 1), specs2(bwd, 1)]
    f = jax.ShapeDtypeStruct((t, RW_WIDTH), F32)
    n_state = 2 * RW_HEADS // RW_GROUP
    gl = RW_GROUP * RW_HEAD_DIM
    return pl.pallas_call(
        functools.partial(_rw_chunk_body, nc_ctx),
        grid=(nc,),
        in_specs=in_specs,
        out_specs=[specs(fwd), specs(bwd)],
        out_shape=[f, f],
        scratch_shapes=[pltpu.VMEM((n_state, gl, gl), F32)],
        compiler_params=_cp(1),
        name="rw_chunks",
    )(r, v, kkn, lw, beta, kd, r, v, kkn, lw, beta, kd)


def _merge_body(n_ctx, t_off, x_ref, mod_ref, g1_ref, g2n_ref, wg_ref, yat_ref, yb_ref, yf_ref, ybw_ref, r_ref, v_ref,
                kd_ref, gd_ref, g2_ref, rk_ref, lng_ref, lnb_ref, seg_ref, woa_ref, wob_ref, wor_ref, wout_ref,
                xmid_ref, h2_ref):
    tm, d = x_ref.shape
    row0 = (pl.program_id(0) + t_off) * tm
    x = x_ref[...]
    mod = mod_ref[...]
    h = _modulated_norm(x, g1_ref[...], mod, row0, n_ctx, 1).astype(BF16)
    gates = jax.nn.sigmoid(_dot(h, wg_ref[...]))
    pa = _dot_tn(yat_ref[...].astype(BF16), woa_ref[...])
    pb = _dot(yb_ref[...].astype(BF16), wob_ref[...])
    seg = seg_ref[...]
    inv = 1.0 / RW_HEAD_DIM
    y = yf_ref[...] + ybw_ref[...]
    dy = y - _seg_sum(y, seg) * inv
    yn = dy * lax.rsqrt(_seg_sum(dy * dy, seg) * inv + RW_GN_EPS) * lng_ref[...] + lnb_ref[...]
    bonus = _seg_sum(r_ref[...] * (kd_ref[0] + kd_ref[1]) * rk_ref[...], seg) * v_ref[...]
    g = _dot(jax.nn.sigmoid(gd_ref[...]).astype(BF16), g2_ref[...])
    pr = _dot(((yn + bonus) * g).astype(BF16), wor_ref[...])
    m = gates[:, :d] * pa + gates[:, d:2 * d] * pb + gates[:, 2 * d:] * pr
    o = _dot(m.astype(BF16), wout_ref[...])
    x_mid = x + _gate_rows(mod, row0, tm, n_ctx, 1) * o
    xmid_ref[...] = x_mid
    h2_ref[...] = _modulated_norm(x_mid, g2n_ref[...], mod, row0, n_ctx, 2).astype(h2_ref.dtype)


def _merge(x_all, mod_l, g1, g2n, wg, yat, yb, yf, ybw, r, v, kd, gd, g2, rk, lng, lnb, seg, woa, wob, wor, wout,
           n_ctx, with_ctx, h2_dtype):
    t, d = x_all.shape
    t_off = 0 if with_ctx else n_ctx // TM
    n_rows = yat.shape[1]
    tok = lambda w: pl.BlockSpec((TM, w), lambda i: (i + t_off, 0))
    own = lambda w: pl.BlockSpec((TM, w), lambda i: (i, 0))
    smalls = (g2, rk, lng, lnb, seg, woa, wob, wor, wout)
    return pl.pallas_call(
        functools.partial(_merge_body, n_ctx, t_off),
        grid=(n_rows // TM,),
        in_specs=[tok(d), _full(mod_l.shape), _full(g1.shape), _full(g2n.shape), _full(wg.shape),
                  pl.BlockSpec((yat.shape[0], TM), lambda i: (0, i)), own(NA_WIDTH),
                  tok(RW_WIDTH), tok(RW_WIDTH), tok(RW_WIDTH), tok(RW_WIDTH),
                  pl.BlockSpec((2, TM, RW_WIDTH), lambda i: (0, i + t_off, 0)), tok(RW_GATE_LORA)]
                 + [_full(a.shape) for a in smalls],
        out_specs=[own(d), own(d)],
        out_shape=[jax.ShapeDtypeStruct((n_rows, d), F32), jax.ShapeDtypeStruct((n_rows, d), h2_dtype)],
        compiler_params=_cp(1),
        name="merge",
    )(x_all, mod_l, g1, g2n, wg, yat, yb, yf, ybw, r, v, kd, gd, *smalls)


def _ffn_body(n_ctx, h_ref, x_ref, mod_ref, w1_ref, w3_ref, w2_ref, o_ref):
    tm = h_ref.shape[0]
    h = h_ref[...]
    acc = jnp.zeros(o_ref.shape, F32)
    for f in range(w1_ref.shape[0]):
        a = _dot(h, w1_ref[f])
        b = _dot(h, w3_ref[f])
        acc = acc + _dot((a * jax.nn.sigmoid(a) * b).astype(BF16), w2_ref[f])
    o_ref[...] = x_ref[...] + _gate_rows(mod_ref[...], pl.program_id(0) * tm, tm, n_ctx, 2) * acc


def _ffn(h2, x_mid, mod_l, w1, w3, w2, n_ctx):
    t, d = x_mid.shape
    tok = pl.BlockSpec((TM, d), lambda i: (i, 0))
    return pl.pallas_call(
        functools.partial(_ffn_body, n_ctx),
        grid=(t // TM,),
        in_specs=[tok, tok, _full(mod_l.shape), _full(w1.shape), _full(w3.shape), _full(w2.shape)],
        out_specs=tok,
        out_shape=jax.ShapeDtypeStruct((t, d), F32),
        compiler_params=_cp(1),
        name="ffn",
    )(h2, x_mid, mod_l, w1, w3, w2)


def _router_body(h_ref, w_ref, idx_ref, gate_ref):
    logits = _dot(h_ref[...].astype(BF16), w_ref[...])
    lane_i = lax.broadcasted_iota(jnp.int32, logits.shape, 1)
    lane = lane_i.astype(F32)
    logits = jnp.where(lane_i < N_EXPERTS, logits, NEG_BIG)
    m1 = jnp.max(logits, axis=-1, keepdims=True)
    i1 = jnp.min(jnp.where(logits == m1, lane, float(LANE)), axis=-1, keepdims=True)
    rest = jnp.where(lane == i1, NEG_BIG, logits)
    m2 = jnp.max(rest, axis=-1, keepdims=True)
    i2 = jnp.min(jnp.where(rest == m2, lane, float(LANE)), axis=-1, keepdims=True)
    e2 = jnp.exp(m2 - m1)
    g1 = 1.0 / (1.0 + e2)
    lane = lane_i
    idx_ref[...] = jnp.where(lane == 0, i1, jnp.where(lane == 1, i2, 0.0)).astype(jnp.int32)
    gate_ref[...] = jnp.where(lane == 0, g1, jnp.where(lane == 1, e2 * g1, 0.0))


def _router(h2, w_router_pad):
    n, d = h2.shape
    tok = lambda w: pl.BlockSpec((TM, w), lambda i: (i, 0))
    return pl.pallas_call(
        _router_body,
        grid=(n // TM,),
        in_specs=[tok(d), _full(w_router_pad.shape)],
        out_specs=[tok(LANE), tok(LANE)],
        out_shape=[jax.ShapeDtypeStruct((n, LANE), jnp.int32), jax.ShapeDtypeStruct((n, LANE), F32)],
        compiler_params=_cp(1),
        name="router",
    )(h2, w_router_pad)


def _row_copy(src_hbm, dst_vmem, src_row, dst_row, sem):
    return pltpu.make_async_copy(src_hbm.at[pl.ds(src_row, 1)], dst_vmem.at[pl.ds(dst_row, 1)], sem)


def _gather_body(idx_hbm, src_hbm, o_ref, idx_smem, sem_idx, sem_rows):
    bm = o_ref.shape[0]
    cp = pltpu.make_async_copy(idx_hbm.at[pl.program_id(0)], idx_smem, sem_idx)
    cp.start()
    cp.wait()

    def start(r, carry):
        _row_copy(src_hbm, o_ref, idx_smem[r], r, sem_rows).start()
        return carry

    lax.fori_loop(0, bm, start, 0)

    def wait(r, carry):
        _row_copy(src_hbm, o_ref, 0, r, sem_rows).wait()
        return carry

    lax.fori_loop(0, bm, wait, 0)


def _gather_rows(idx, src, bm):
    nb = idx.shape[0]
    width = src.shape[1]
    return pl.pallas_call(
        _gather_body,
        grid=(nb,),
        in_specs=[pl.BlockSpec(memory_space=pl.ANY), pl.BlockSpec(memory_space=pl.ANY)],
        out_specs=pl.BlockSpec((bm, width), lambda b: (b, 0)),
        out_shape=jax.ShapeDtypeStruct((nb * bm, width), src.dtype),
        scratch_shapes=[pltpu.SMEM((bm,), jnp.int32), pltpu.SemaphoreType.DMA(()), pltpu.SemaphoreType.DMA(())],
        compiler_params=_cp(1),
        name="gather_rows",
    )(idx, src)


def _moe_ffn_body(be_ref, nb_ref, x_ref, w1_ref, w3_ref, w2_ref, o_ref, acc_ref):
    b, f = pl.program_id(0), pl.program_id(1)
    nf = pl.num_programs(1)

    @pl.when(f == 0)
    def _():
        acc_ref[...] = jnp.zeros_like(acc_ref)

    @pl.when(b < nb_ref[0])
    def _():
        x = x_ref[...].astype(BF16)
        a = _dot(x, w1_ref[0])
        g = _dot(x, w3_ref[0])
        acc_ref[...] += _dot((a * jax.nn.sigmoid(a) * g).astype(BF16), w2_ref[0])

    @pl.when(f == nf - 1)
    def _():
        o_ref[...] = acc_ref[...]


def _moe_ffn(blk_expert, n_used, xg, w1, w3, w2, fc):
    n_slots, d = xg.shape
    nb = n_slots // MOE_BM
    nf = w1.shape[2] // fc
    grid_spec = pltpu.PrefetchScalarGridSpec(
        num_scalar_prefetch=2,
        grid=(nb, nf),
        in_specs=[pl.BlockSpec((MOE_BM, d), lambda b, f, be, nu: (b, 0)),
                  pl.BlockSpec((1, d, fc), lambda b, f, be, nu: (be[b], 0, f)),
                  pl.BlockSpec((1, d, fc), lambda b, f, be, nu: (be[b], 0, f)),
                  pl.BlockSpec((1, fc, d), lambda b, f, be, nu: (be[b], f, 0))],
        out_specs=pl.BlockSpec((MOE_BM, d), lambda b, f, be, nu: (b, 0)),
        scratch_shapes=[pltpu.VMEM((MOE_BM, d), F32)],
    )
    return pl.pallas_call(
        _moe_ffn_body,
        grid_spec=grid_spec,
        out_shape=jax.ShapeDtypeStruct((n_slots, d), F32),
        compiler_params=_cp(2),
        name="moe_ffn",
    )(blk_expert, n_used, xg, w1, w3, w2)


def _combine_body(y0_ref, y1_ref, gate_ref, x_ref, mod_ref, o_ref):
    d = x_ref.shape[1]
    g = gate_ref[...]
    f = g[:, 0:1] * y0_ref[...] + g[:, 1:2] * y1_ref[...]
    o_ref[...] = x_ref[...] + mod_ref[0:1, 5 * d:6 * d] * f


def _moe_combine(yk, gate_pad, x_mid, mod_l):
    n, d = x_mid.shape
    nt = n // TM
    return pl.pallas_call(
        _combine_body,
        grid=(nt,),
        in_specs=[pl.BlockSpec((TM, d), lambda i: (i, 0)), pl.BlockSpec((TM, d), lambda i: (i + nt, 0)),
                  pl.BlockSpec((TM, LANE), lambda i: (i, 0)), pl.BlockSpec((TM, d), lambda i: (i, 0)),
                  _full(mod_l.shape)],
        out_specs=pl.BlockSpec((TM, d), lambda i: (i, 0)),
        out_shape=jax.ShapeDtypeStruct((n, d), F32),
        compiler_params=_cp(1),
        name="moe_combine",
    )(yk, yk, gate_pad, x_mid, mod_l)


def _moe(h2, x_mid, mod_l, router_w, w1, w3, w2, fc):
    n, d = h2.shape
    e_n = router_w.shape[1]
    idx_pad, gate_pad = _router(h2, jnp.zeros((d, LANE), BF16).at[:, :e_n].set(router_w.astype(BF16)))
    flat_e = idx_pad[:, :TOP_K].T.reshape(-1)
    onehot = (flat_e[:, None] == jnp.arange(e_n)[None, :]).astype(jnp.int32)
    rank = jnp.take_along_axis(jnp.cumsum(onehot, axis=0) - onehot, flat_e[:, None], axis=1)[:, 0]
    counts = jnp.sum(onehot, axis=0)
    padded = (counts + MOE_BM - 1) // MOE_BM * MOE_BM
    pad_ends = jnp.cumsum(padded)
    slot = ((pad_ends - padded)[flat_e] + rank).astype(jnp.int32)
    nb = TOP_K * n // MOE_BM + e_n
    n_slots = nb * MOE_BM
    tok_of_slot = jnp.zeros((n_slots,), jnp.int32).at[slot].set(jnp.tile(jnp.arange(n, dtype=jnp.int32), TOP_K))
    blk_expert = jnp.minimum(jnp.searchsorted(pad_ends, jnp.arange(nb, dtype=jnp.int32) * MOE_BM, side="right"),
                             e_n - 1).astype(jnp.int32)
    n_used = (pad_ends[-1:] // MOE_BM).astype(jnp.int32)
    xg = _gather_rows(tok_of_slot.reshape(nb, MOE_BM), h2, MOE_BM)
    yg = _moe_ffn(blk_expert, n_used, xg, w1, w3, w2, fc)
    yk = _gather_rows(slot.reshape(-1, MOE_BM), yg, MOE_BM)
    return _moe_combine(yk, gate_pad, x_mid, mod_l)


def _rope_tables(n_ctx, n_lat):
    pos = jnp.arange(n_lat)
    half = MLA_ROPE // 4
    freqs = jnp.exp(-math.log(ROPE_BASE) * jnp.arange(half, dtype=F32) / half)
    ang_r = (pos // GRID_W).astype(F32)[:, None] * freqs[None, :]
    ang_c = (pos % GRID_W).astype(F32)[:, None] * freqs[None, :]
    one = jnp.ones((n_lat, MLA_NOPE), F32)
    zero = jnp.zeros((n_lat, MLA_NOPE), F32)
    tail1 = jnp.ones((n_lat, HEAD_PAD - MLA_QK), F32)
    tail0 = jnp.zeros((n_lat, HEAD_PAD - MLA_QK), F32)
    z8 = jnp.zeros((n_lat, half), F32)
    cr, sr, cc, sc = jnp.cos(ang_r), jnp.sin(ang_r), jnp.cos(ang_c), jnp.sin(ang_c)
    cos = jnp.concatenate([one, cr, cr, cc, cc, tail1], axis=1)
    sin_a = jnp.concatenate([zero, -sr, z8, -sc, z8, tail0], axis=1)
    sin_b = jnp.concatenate([zero, z8, sr, z8, sc, tail0], axis=1)
    ctx_pad = lambda a, v: jnp.concatenate([jnp.full((n_ctx, HEAD_PAD), v, F32), a], axis=0)
    return ctx_pad(cos, 1.0), ctx_pad(sin_a, 0.0), ctx_pad(sin_b, 0.0)


def _seg_matrix(width, seg):
    i = np.arange(width)
    return jnp.asarray((i[:, None] // seg == i[None, :] // seg).astype(np.float32), dtype=BF16)


def _block_diag2(w):
    _, r, c = w.shape
    z = jnp.zeros((r, c), w.dtype)
    return jnp.concatenate([jnp.concatenate([w[0], z], axis=1), jnp.concatenate([z, w[1]], axis=1)], axis=0)


def _pad_heads(w, n_heads, width, pad_to):
    rows = w.shape[0]
    w = w.reshape(rows, n_heads, width)
    return jnp.pad(w, ((0, 0), (0, 0), (0, pad_to - width))).reshape(rows, n_heads * pad_to)


def kernel(x, c, ctx, c_ctx, mod_w, mod_b, norm1_g, norm2_g, w_in, mla_cq_g, mla_wuq, mla_ckv_g, mla_wukv, mla_qn_g, mla_kn_g, mla_wo, na_qn_g, na_kn_g, na_rpb, na_wo, rw_mu, rw_w0, rw_w2, rw_a0, rw_a2, rw_g2, rw_kk, rw_ka, rw_rk, rw_ln_g, rw_ln_b, rw_wo, w_out, ffn_w1, ffn_w3, ffn_w2, moe_router, moe_w1, moe_w3, moe_w2):
    b, n_lat, d = x.shape
    n_ctx = ctx.shape[1]
    depth = mod_w.shape[0]
    assert b == 1 and n_ctx % TM == 0 and n_lat % TM == 0 and n_lat % (NA_ROWS * GRID_W) == 0
    assert n_lat // GRID_W >= NA_KROWS and n_ctx % RW_CHUNK == 0

    mod = _modulation(c, c_ctx, mod_w, mod_b)
    x_all = jnp.concatenate([ctx[0], x[0]], axis=0)
    rope_tabs = _rope_tables(n_ctx, n_lat)
    seg64 = _seg_matrix(RW_WIDTH, RW_HEAD_DIM)

    o_ckv = MLA_Q_LORA
    o_kr = o_ckv + MLA_KV_LORA
    o_na = o_kr + MLA_ROPE
    o_rz = o_na + 3 * NA_WIDTH
    o_gate = o_rz + RW_IN

    for l in range(depth):
        need_ctx = l < depth - 1
        mod_l = mod[l]
        g1 = norm1_g[l][None, :]
        g2n = norm2_g[l][None, :]
        wi = w_in[l]

        w_a = jnp.pad(wi[:, :o_na], ((0, 0), (0, 6 * LANE - o_na))).astype(BF16)
        w_na = wi[:, o_na:o_rz].astype(BF16)
        w_rz = wi[:, o_rz:o_gate].astype(BF16)
        w_g = wi[:, o_gate:].astype(BF16)
        wuq = _pad_heads(mla_wuq[l], MLA_HEADS, MLA_QK, HEAD_PAD).astype(BF16)
        wukv = mla_wukv[l].reshape(MLA_KV_LORA, MLA_HEADS, MLA_NOPE + MLA_V)
        wuk = jnp.pad(wukv[:, :, :MLA_NOPE], ((0, 0), (0, 0), (0, HEAD_PAD - MLA_NOPE))).reshape(MLA_KV_LORA, -1).astype(BF16)
        wuvt = wukv[:, :, MLA_NOPE:].reshape(MLA_KV_LORA, -1).T.astype(BF16)
        qng = jnp.pad(mla_qn_g[l], (0, HEAD_PAD - MLA_QK))[None, :]
        kng = jnp.pad(mla_kn_g[l], (0, HEAD_PAD - MLA_QK))[None, :]
        bound = 1.01 * math.sqrt(MLA_QK) * jnp.max(jnp.abs(mla_qn_g[l])) * jnp.max(jnp.abs(mla_kn_g[l]))
        shift = jnp.zeros((1, HEAD_PAD), F32).at[0, MLA_QK].set(-bound * math.log2(math.e))
        mla_wts = (w_a, mla_cq_g[l][None, :], wuq, mla_ckv_g[l][None, :], wuk, wuvt, qng, kng, shift)

        qa, ka, vta = _mla_proj(x_all, mod_l, g1, mla_wts, rope_tabs, n_ctx)
        nt_ctx, nt_all = n_ctx // TM, (n_ctx + n_lat) // TM

        def mla_attention(fast):
            def run(qa, ka, vta):
                y = _mla_attn(qa, ka, vta, nt_ctx, nt_all - nt_ctx, nt_all, fast)
                if need_ctx:
                    y = jnp.concatenate([_mla_attn(qa, ka, vta, 0, nt_ctx, nt_ctx, fast), y], axis=1)
                return y
            return run

        yat = lax.cond(bound <= MLA_FAST_BOUND, mla_attention(True), mla_attention(False), qa, ka, vta)

        qb, kb, vb = _na_proj(x_all, mod_l, g1, w_na, jnp.tile(na_qn_g[l], NA_HEADS)[None, :],
                              jnp.tile(na_kn_g[l], NA_HEADS)[None, :], seg64, n_ctx)
        yb = _na_attn(qb, kb, vb, _na_bias_tables(na_rpb[l]), n_ctx)
        if need_ctx:
            yb = jnp.concatenate([_ctx_attn(qb, kb, vb, n_ctx), yb], axis=0)

        z = _rz_proj(x_all, mod_l, g1, w_rz, n_ctx)
        r, v, kkn, lw, beta, kd, gd = _rw_prep(
            z, rw_mu[l], rw_w0[l].reshape(1, -1), _block_diag2(rw_w2[l]).astype(BF16), rw_a0[l].reshape(1, -1),
            _block_diag2(rw_a2[l]).astype(BF16), rw_kk[l][None, :], rw_ka[l][None, :], seg64, n_ctx)
        yf, ybw = _rw_chunks(r, v, kkn, lw, beta, kd, n_ctx)

        x_mid, h2 = _merge(x_all, mod_l, g1, g2n, w_g, yat, yb, yf, ybw, r, v, kd, gd,
                           rw_g2[l].astype(BF16), rw_rk[l].reshape(1, -1), rw_ln_g[l][None, :], rw_ln_b[l][None, :], seg64,
                           mla_wo[l].astype(BF16), na_wo[l].astype(BF16), rw_wo[l].astype(BF16), w_out[l].astype(BF16),
                           n_ctx, need_ctx, BF16 if l % 2 == 0 else F32)

        if l % 2 == 0:
            fc = FFN_FC
            w1 = ffn_w1[l // 2]
            nf = w1.shape[1] // fc
            w1r = w1.reshape(d, nf, fc).transpose(1, 0, 2).astype(BF16)
            w3r = ffn_w3[l // 2].reshape(d, nf, fc).transpose(1, 0, 2).astype(BF16)
            w2r = ffn_w2[l // 2].reshape(nf, fc, d).astype(BF16)
            if need_ctx:
                x_all = _ffn(h2, x_mid, mod_l, w1r, w3r, w2r, n_ctx)
            else:
                x_lat = _ffn(h2, x_mid, mod_l, w1r, w3r, w2r, 0)
        else:
            moe_args = (moe_router[l // 2], moe_w1[l // 2].astype(BF16), moe_w3[l // 2].astype(BF16),
                        moe_w2[l // 2].astype(BF16), MOE_FC)
            if need_ctx:
                lat = _moe(h2[n_ctx:], x_mid[n_ctx:], mod_l, *moe_args)
                ctx_rows = _moe(h2[:n_ctx], x_mid[:n_ctx], mod_l.at[0].set(mod_l[1]), *moe_args)
                x_all = jnp.concatenate([ctx_rows, lat], axis=0)
            else:
                x_lat = _moe(h2, x_mid, mod_l, *moe_args)
    return x_lat[None]
```

```python
import functools
import math

import numpy as np
import jax
import jax.numpy as jnp
from jax import lax
from jax.experimental import pallas as pl
from jax.experimental.pallas import tpu as pltpu

F32 = jnp.float32
BF16 = jnp.bfloat16

GRID_W = 64
MLA_HEADS, MLA_NOPE, MLA_ROPE, MLA_V = 8, 64, 32, 64
MLA_QK = MLA_NOPE + MLA_ROPE
MLA_Q_LORA, MLA_KV_LORA = 384, 256
NA_HEADS, NA_HEAD_DIM = 8, 64
NA_WIDTH = NA_HEADS * NA_HEAD_DIM
NA_WIN_ROWS, NA_WIN_COLS = 8, 16
RW_HEADS, RW_HEAD_DIM = 8, 64
RW_WIDTH = RW_HEADS * RW_HEAD_DIM
RW_DECAY_LORA, RW_ICLR_LORA, RW_GATE_LORA = 64, 64, 128
RW_GN_EPS = 64e-5
RW_IN = 3 * RW_WIDTH + 2 * RW_DECAY_LORA + 2 * RW_ICLR_LORA + RW_GATE_LORA
N_EXPERTS, TOP_K = 8, 2
ROPE_BASE = 10000.0
NORM_EPS = 1e-6

LANE = 128
SUBLANE = 8
VMEM_LIMIT = 56 * 1024 * 1024

TM = 256
HEAD_PAD = LANE
MLA_V_ROWS = 80
MLA_Q_STREAMS = 2
MLA_FAST_UNROLL = 65
MLA_KEY_TILES_PER_STEP = 5
MLA_FAST_BOUND = 40.0
RW_CHUNK = 64
RW_GROUP = 4
RW_CHUNKS_PER_STEP = 2
NA_ROWS = 4
NA_KROWS = NA_ROWS + NA_WIN_ROWS
MOE_BM = 512
MOE_FC = 1792
GATHER_UNROLL = 16
FFN_FC = 256
NEG_BIG = -1e30


def _cp(n_axes, vmem=VMEM_LIMIT):
    return pltpu.CompilerParams(dimension_semantics=("arbitrary",) * n_axes, vmem_limit_bytes=vmem)


def _dot(a, b):
    return jnp.dot(a, b, preferred_element_type=F32)


def _dot_nt(a, b):
    return lax.dot_general(a, b, (((1,), (1,)), ((), ())), preferred_element_type=F32)


def _dot_tn(a, b):
    return lax.dot_general(a, b, (((0,), (0,)), ((), ())), preferred_element_type=F32)


def _split2(x):
    hi = x.astype(BF16)
    lo = (x - hi.astype(F32)).astype(BF16)
    return hi, lo


def _split3(x):
    hi = x.astype(BF16)
    r1 = x - hi.astype(F32)
    mid = r1.astype(BF16)
    lo = (r1 - mid.astype(F32)).astype(BF16)
    return hi, mid, lo


def _seg_sum(x, seg01):
    hi, lo = _split2(x)
    return _dot(hi, seg01) + _dot(lo, seg01)


def _full(shape):
    nd = len(shape)
    return pl.BlockSpec(shape, lambda *_: (0,) * nd)


def _modulated_norm(x, g, mod, first_row, n_ctx, which):
    d = x.shape[1]
    off = 0 if which == 1 else 3 * d
    y = x * lax.rsqrt(jnp.mean(x * x, axis=-1, keepdims=True) + NORM_EPS) * g
    rows = first_row + lax.broadcasted_iota(jnp.int32, (x.shape[0], 1), 0)
    is_ctx = rows < n_ctx
    sh = jnp.where(is_ctx, mod[1:2, off:off + d], mod[0:1, off:off + d])
    sc = jnp.where(is_ctx, mod[1:2, off + d:off + 2 * d], mod[0:1, off + d:off + 2 * d])
    return y * (1.0 + sc) + sh


def _gate_rows(mod, first_row, n_rows, n_ctx, which):
    d = mod.shape[1] // 6
    off = 2 * d if which == 1 else 5 * d
    rows = first_row + lax.broadcasted_iota(jnp.int32, (n_rows, 1), 0)
    return jnp.where(rows < n_ctx, mod[1:2, off:off + d], mod[0:1, off:off + d])


def _mod_body(cc_ref, w_ref, b_ref, o_ref):
    cc = cc_ref[...]
    s = cc * jax.nn.sigmoid(cc)
    o_ref[0] = _dot(s.astype(BF16), w_ref[0].astype(BF16)) + b_ref[0]


def _modulation(c, c_ctx, mod_w, mod_b):
    depth, d, d6 = mod_w.shape
    nt = d6 // 4
    cc = jnp.zeros((SUBLANE, d), F32).at[0].set(c[0]).at[1].set(c_ctx)
    return pl.pallas_call(
        _mod_body,
        grid=(depth, d6 // nt),
        in_specs=[_full((SUBLANE, d)),
                  pl.BlockSpec((1, d, nt), lambda l, j: (l, 0, j)),
                  pl.BlockSpec((1, 1, nt), lambda l, j: (l, 0, j))],
        out_specs=pl.BlockSpec((1, SUBLANE, nt), lambda l, j: (l, 0, j)),
        out_shape=jax.ShapeDtypeStruct((depth, SUBLANE, d6), F32),
        compiler_params=_cp(2),
        name="modulation",
    )(cc, mod_w, mod_b.reshape(depth, 1, d6))


def _rms(x, width):
    return x * lax.rsqrt(jnp.sum(x * x, axis=-1, keepdims=True) * (1.0 / width) + NORM_EPS)


def _rope(x, cos, sin_a, sin_b):
    return x * cos + pltpu.roll(x, HEAD_PAD - 8, 1) * sin_a + pltpu.roll(x, 8, 1) * sin_b


def _mla_proj_body(n_ctx, x_ref, mod_ref, g1_ref, wa_ref, cqg_ref, wuq_ref, ckvg_ref, wuk_ref, wuvt_ref,
                   qng_ref, kng_ref, shift_ref, cos_ref, sa_ref, sb_ref, q_ref, k_ref, vt_ref):
    tm = x_ref.shape[0]
    h = _modulated_norm(x_ref[...], g1_ref[...], mod_ref[...], pl.program_id(0) * tm, n_ctx, 1).astype(BF16)
    z = _dot(h, wa_ref[...])
    cq = z[:, :MLA_Q_LORA]
    ckv = z[:, MLA_Q_LORA:MLA_Q_LORA + MLA_KV_LORA]
    kr = z[:, MLA_Q_LORA + MLA_KV_LORA:]
    cqn = (_rms(cq, MLA_Q_LORA) * cqg_ref[...]).astype(BF16)
    ckvn = (_rms(ckv, MLA_KV_LORA) * ckvg_ref[...]).astype(BF16)
    q = _dot(cqn, wuq_ref[...])
    kn = _dot(ckvn, wuk_ref[...])
    vt = _dot_nt(wuvt_ref[...], ckvn)
    kr_sh = pltpu.roll(kr, MLA_NOPE, 1)
    cos, sa, sb = cos_ref[...], sa_ref[...], sb_ref[...]
    q_scale = MLA_QK ** -0.5 * math.log2(math.e)
    one_lane = jnp.where(lax.broadcasted_iota(jnp.int32, (1, HEAD_PAD), 1) == MLA_QK, 1.0, 0.0)
    pad_rows = MLA_V_ROWS - MLA_V
    ones_rows = jnp.where(lax.broadcasted_iota(jnp.int32, (pad_rows, tm), 0) == 0, 1.0, 0.0).astype(BF16)
    for hh in range(MLA_HEADS):
        sl = slice(hh * HEAD_PAD, (hh + 1) * HEAD_PAD)
        qh = _rope(_rms(q[:, sl], MLA_QK) * qng_ref[...], cos, sa, sb) * q_scale + shift_ref[...]
        kh = _rope(_rms(kn[:, sl] + kr_sh, MLA_QK) * kng_ref[...], cos, sa, sb) + one_lane
        q_ref[:, sl] = qh.astype(BF16)
        k_ref[:, sl] = kh.astype(BF16)
        vt_ref[0, hh * MLA_V_ROWS:hh * MLA_V_ROWS + MLA_V, :] = vt[hh * MLA_V:(hh + 1) * MLA_V].astype(BF16)
        vt_ref[0, hh * MLA_V_ROWS + MLA_V:(hh + 1) * MLA_V_ROWS, :] = ones_rows


def _mla_proj(x_all, mod_l, g1, wts, rope_tabs, n_ctx):
    t, d = x_all.shape
    nt = t // TM
    hp = MLA_HEADS * HEAD_PAD
    tok = lambda w: pl.BlockSpec((TM, w), lambda i: (i, 0))
    return pl.pallas_call(
        functools.partial(_mla_proj_body, n_ctx),
        grid=(nt,),
        in_specs=[tok(d), _full(mod_l.shape), _full(g1.shape)] + [_full(w.shape) for w in wts]
                 + [tok(HEAD_PAD), tok(HEAD_PAD), tok(HEAD_PAD)],
        out_specs=[tok(hp), tok(hp), pl.BlockSpec((1, MLA_HEADS * MLA_V_ROWS, TM), lambda i: (i, 0, 0))],
        out_shape=[jax.ShapeDtypeStruct((t, hp), BF16), jax.ShapeDtypeStruct((t, hp), BF16),
                   jax.ShapeDtypeStruct((nt, MLA_HEADS * MLA_V_ROWS, TM), BF16)],
        compiler_params=_cp(1),
        name="mla_proj",
    )(x_all, mod_l, g1, *wts, *rope_tabs)


def _mla_attn_fast_body(unroll, n_streams, *refs):
    q_refs, (k_ref, vt_ref, o_ref) = refs[:n_streams], refs[n_streams:]
    tq = q_refs[0].shape[0]
    nk, rows_v, tk = vt_ref.shape
    qts = [q_ref[...].astype(F32).T.astype(BF16) for q_ref in q_refs]

    def scores(j):
        j = jnp.minimum(j, nk - 1)
        kj = k_ref[pl.ds(pl.multiple_of(j * tk, tk), tk), :]
        return tuple(_dot(kj, qt) for qt in qts)

    def probs(ss):
        return tuple(jnp.exp2(s).astype(BF16) for s in ss)

    def step(it, carry):
        accs, ss, ps = carry
        for u in range(unroll):
            j = it * unroll + u
            s_new = scores(j + 2)
            p_new = probs(ss)
            vj = vt_ref[j]
            accs = tuple(a + _dot(vj, p) for a, p in zip(accs, ps))
            ss, ps = s_new, p_new
        return accs, ss, ps

    init = (tuple(jnp.zeros((rows_v, tq), F32) for _ in range(n_streams)), scores(1), probs(scores(0)))
    accs, _, _ = lax.fori_loop(0, nk // unroll, step, init)
    for i in range(n_streams):
        o_ref[:, i * tq:(i + 1) * tq] = accs[i][:MLA_V] / accs[i][MLA_V:MLA_V + 1]


def _mla_attn_online_body(kb, q_ref, k_ref, vt_ref, o_ref):
    tq = q_ref.shape[0]
    nk, _, tk = vt_ref.shape
    nblk = nk // kb
    rows = kb * tk
    q = q_ref[...]

    def scores(j):
        return _dot_nt(k_ref[pl.ds(pl.multiple_of(j * rows, rows), rows), :], q)

    def consume(j, m, l, acc, s):
        m_new = jnp.maximum(m, jnp.max(s, axis=0, keepdims=True))
        alpha = jnp.exp2(m - m_new)
        p = jnp.exp2(s - m_new)
        l = alpha * l + jnp.sum(p, axis=0, keepdims=True)
        pb = p.astype(BF16)
        pv = _dot(vt_ref[j * kb][:MLA_V], pb[0:tk])
        for i in range(1, kb):
            pv = pv + _dot(vt_ref[j * kb + i][:MLA_V], pb[i * tk:(i + 1) * tk])
        return m_new, l, alpha * acc + pv

    def step(j, carry):
        m, l, acc, s = carry
        s_next = scores(j + 1)
        m, l, acc = consume(j, m, l, acc, s)
        return m, l, acc, s_next

    init = (jnp.full((1, tq), NEG_BIG, F32), jnp.zeros((1, tq), F32), jnp.zeros((MLA_V, tq), F32), scores(0))
    m, l, acc, s = lax.fori_loop(0, nblk - 1, step, init)
    _, l, acc = consume(nblk - 1, m, l, acc, s)
    o_ref[...] = acc / l


def _largest_divisor(n, cap):
    return max(b for b in range(1, cap + 1) if n % b == 0)


def _mla_attn(q, k, vt, q_tile0, nq, nk, fast):
    if fast:
        ns = MLA_Q_STREAMS if nq % MLA_Q_STREAMS == 0 else 1
        body = functools.partial(_mla_attn_fast_body, _largest_divisor(nk, MLA_FAST_UNROLL), ns)
    else:
        ns = 1
        body = functools.partial(_mla_attn_online_body, _largest_divisor(nk, MLA_KEY_TILES_PER_STEP))
    q_specs = [pl.BlockSpec((TM, HEAD_PAD), lambda h, i, s=s: (ns * i + q_tile0 + s, h)) for s in range(ns)]
    return pl.pallas_call(
        body,
        grid=(MLA_HEADS, nq // ns),
        in_specs=q_specs + [pl.BlockSpec((nk * TM, HEAD_PAD), lambda h, i: (0, h)),
                            pl.BlockSpec((nk, MLA_V_ROWS, TM), lambda h, i: (0, h, 0))],
        out_specs=pl.BlockSpec((MLA_V, ns * TM), lambda h, i: (h, i)),
        out_shape=jax.ShapeDtypeStruct((MLA_HEADS * MLA_V, nq * TM), F32),
        compiler_params=_cp(2),
        name="mla_attn_fast" if fast else "mla_attn_online",
    )(*([q] * ns), k, vt)


def _na_proj_body(n_ctx, x_ref, mod_ref, g1_ref, w_ref, qg_ref, kg_ref, seg_ref, q_ref, k_ref, v_ref):
    tm = x_ref.shape[0]
    h = _modulated_norm(x_ref[...], g1_ref[...], mod_ref[...], pl.program_id(0) * tm, n_ctx, 1).astype(BF16)
    z = _dot(h, w_ref[...])
    seg = seg_ref[...]
    inv = 1.0 / NA_HEAD_DIM

    def head_norm(y, g):
        return y * lax.rsqrt(_seg_sum(y * y, seg) * inv + NORM_EPS) * g

    q_ref[...] = (head_norm(z[:, :NA_WIDTH], qg_ref[...]) * (NA_HEAD_DIM ** -0.5)).astype(BF16)
    k_ref[...] = head_norm(z[:, NA_WIDTH:2 * NA_WIDTH], kg_ref[...]).astype(BF16)
    v_ref[...] = z[:, 2 * NA_WIDTH:].astype(BF16)


def _na_proj(x_all, mod_l, g1, w_na, qg, kg, seg, n_ctx):
    t, d = x_all.shape
    tok = lambda w: pl.BlockSpec((TM, w), lambda i: (i, 0))
    return pl.pallas_call(
        functools.partial(_na_proj_body, n_ctx),
        grid=(t // TM,),
        in_specs=[tok(d), _full(mod_l.shape), _full(g1.shape), _full(w_na.shape), _full(qg.shape), _full(kg.shape),
                  _full(seg.shape)],
        out_specs=[tok(NA_WIDTH)] * 3,
        out_shape=[jax.ShapeDtypeStruct((t, NA_WIDTH), BF16)] * 3,
        compiler_params=_cp(1),
        name="na_proj",
    )(x_all, mod_l, g1, w_na, qg, kg, seg)


def _head_masks():
    lane = lax.broadcasted_iota(jnp.int32, (1, LANE), 1)
    return [(lane // NA_HEAD_DIM == h) for h in range(LANE // NA_HEAD_DIM)]


def _na_attn_body(n_ctx, n_grid_rows, q_ref, k_ref, v_ref, bias_ref, o_ref):
    nq = q_ref.shape[0]
    nkl = NA_KROWS * GRID_W
    g = pl.program_id(1)
    kr0 = jnp.clip(g * NA_ROWS - NA_WIN_ROWS // 2, 0, n_grid_rows - NA_KROWS)
    start = pl.multiple_of(n_ctx + kr0 * GRID_W, GRID_W)
    k_loc = k_ref[pl.ds(start, nkl), :]
    v_loc = v_ref[pl.ds(start, nkl), :]
    k_ctx = k_ref[0:n_ctx, :]
    v_ctx = v_ref[0:n_ctx, :]
    q = q_ref[...]
    out = jnp.zeros((nq, LANE), F32)
    for h, hm in enumerate(_head_masks()):
        qh = jnp.where(hm, q, jnp.zeros_like(q))
        s_loc = _dot_nt(qh, k_loc) + bias_ref[0, h]
        s_ctx = _dot_nt(qh, k_ctx)
        m = jnp.maximum(jnp.max(s_loc, axis=-1, keepdims=True), jnp.max(s_ctx, axis=-1, keepdims=True))
        p_loc = jnp.exp(s_loc - m)
        p_ctx = jnp.exp(s_ctx - m)
        l = jnp.sum(p_loc, axis=-1, keepdims=True) + jnp.sum(p_ctx, axis=-1, keepdims=True)
        o = (_dot(p_loc.astype(BF16), v_loc) + _dot(p_ctx.astype(BF16), v_ctx)) / l
        out = jnp.where(hm, o, out)
    o_ref[...] = out


def _na_attn(q, k, v, bias, n_ctx):
    t = q.shape[0]
    n_lat = t - n_ctx
    n_grid_rows = n_lat // GRID_W
    nq = NA_ROWS * GRID_W
    ng = n_grid_rows // NA_ROWS
    q_blk0 = n_ctx // nq
    npairs = NA_WIDTH // LANE

    def case(g):
        return jnp.where(g == 0, 0, jnp.where(g == ng - 1, 2, 1))

    return pl.pallas_call(
        functools.partial(_na_attn_body, n_ctx, n_grid_rows),
        grid=(npairs, ng),
        in_specs=[pl.BlockSpec((nq, LANE), lambda p, g: (g + q_blk0, p)),
                  pl.BlockSpec((t, LANE), lambda p, g: (0, p)),
                  pl.BlockSpec((t, LANE), lambda p, g: (0, p)),
                  pl.BlockSpec((1, LANE // NA_HEAD_DIM, nq, NA_KROWS * GRID_W), lambda p, g: (case(g), p, 0, 0))],
        out_specs=pl.BlockSpec((nq, LANE), lambda p, g: (g, p)),
        out_shape=jax.ShapeDtypeStruct((n_lat, NA_WIDTH), F32),
        compiler_params=_cp(2),
        name="na_attn",
    )(q, k, v, bias)


def _ctx_attn_body(q_ref, k_ref, v_ref, o_ref):
    q, k, v = q_ref[...], k_ref[...], v_ref[...]
    out = jnp.zeros(o_ref.shape, F32)
    for hm in _head_masks():
        s = _dot_nt(jnp.where(hm, q, jnp.zeros_like(q)), k)
        p = jnp.exp(s - jnp.max(s, axis=-1, keepdims=True))
        o = _dot(p.astype(BF16), v) / jnp.sum(p, axis=-1, keepdims=True)
        out = jnp.where(hm, o, out)
    o_ref[...] = out


def _ctx_attn(q, k, v, n_ctx):
    npairs = NA_WIDTH // LANE
    blk = pl.BlockSpec((n_ctx, LANE), lambda p: (0, p))
    return pl.pallas_call(
        _ctx_attn_body,
        grid=(npairs,),
        in_specs=[blk, blk, blk],
        out_specs=blk,
        out_shape=jax.ShapeDtypeStruct((n_ctx, NA_WIDTH), F32),
        compiler_params=_cp(1),
        name="ctx_attn",
    )(q, k, v)


def _na_bias_tables(rpb):
    h = rpb.shape[0]
    qc = np.arange(GRID_W)[:, None]
    kc = np.arange(GRID_W)[None, :]
    col_start = np.clip(qc - NA_WIN_COLS // 2, 0, GRID_W - NA_WIN_COLS)
    col_ok = (kc >= col_start) & (kc < col_start + NA_WIN_COLS)
    sel = (kc - qc + NA_WIN_COLS - 1)[None] == np.arange(2 * NA_WIN_COLS - 1)[:, None, None]
    by_row = jnp.einsum("hrd,dqk->hrqk", rpb.astype(F32), jnp.asarray(sel & col_ok[None], F32),
                        precision=lax.Precision.HIGHEST)
    by_row = jnp.where(col_ok[None, None], by_row, NEG_BIG)
    masked = jnp.full((h, GRID_W, GRID_W), NEG_BIG, F32)
    tabs = []
    for off, first_row in ((0, lambda qr: 0), (NA_WIN_ROWS // 2, lambda qr: qr), (NA_WIN_ROWS, lambda qr: NA_ROWS)):
        q_rows = []
        for qr in range(NA_ROWS):
            lo = first_row(qr)
            blocks = [by_row[:, kr - off - qr + NA_WIN_ROWS - 1] if lo <= kr < lo + NA_WIN_ROWS else masked
                      for kr in range(NA_KROWS)]
            q_rows.append(jnp.concatenate(blocks, axis=2))
        tabs.append(jnp.concatenate(q_rows, axis=1))
    return jnp.stack(tabs)


def _rz_proj_body(n_ctx, x_ref, mod_ref, g1_ref, w_ref, z_ref):
    tm = x_ref.shape[0]
    h = _modulated_norm(x_ref[...], g1_ref[...], mod_ref[...], pl.program_id(0) * tm, n_ctx, 1).astype(BF16)
    z_ref[...] = _dot(h, w_ref[...])


def _rz_proj(x_all, mod_l, g1, w_rz, n_ctx):
    t, d = x_all.shape
    tok = lambda w: pl.BlockSpec((TM, w), lambda i: (i, 0))
    return pl.pallas_call(
        functools.partial(_rz_proj_body, n_ctx),
        grid=(t // TM,),
        in_specs=[tok(d), _full(mod_l.shape), _full(g1.shape), _full(w_rz.shape)],
        out_specs=tok(RW_IN),
        out_shape=jax.ShapeDtypeStruct((t, RW_IN), F32),
        compiler_params=_cp(1),
        name="rz_proj",
    )(x_all, mod_l, g1, w_rz)


def _rw_prep_body(n_ctx, n_tok, z_ref, zp_ref, zn_ref, mu_ref, w0_ref, w2_ref, a0_ref, a2_ref, kk_ref, ka_ref, seg_ref,
                  r_ref, v_ref, kkn_ref, lw_ref, beta_ref, kd_ref, gd_ref):
    tm = z_ref.shape[0]
    z = z_ref[...]
    loc = lax.broadcasted_iota(jnp.int32, (tm, 1), 0)
    rows = pl.program_id(0) * tm + loc
    zp = jnp.where(loc == 0, zp_ref[0][SUBLANE - 1:SUBLANE, :], pltpu.roll(z, 1, 0))
    zp = jnp.where((rows == 0) | (rows == n_ctx), 0.0, zp)
    zn = jnp.where(loc == tm - 1, zn_ref[0][0:1, :], pltpu.roll(z, tm - 1, 0))
    zn = jnp.where((rows == n_ctx - 1) | (rows == n_tok - 1), 0.0, zn)
    mu = mu_ref[...]
    zs = z + mu[0:1] * (zp - z) + mu[1:2] * (zn - z)
    w = RW_WIDTH
    r, k, v = zs[:, :w], zs[:, w:2 * w], zs[:, 2 * w:3 * w]
    wd = zs[:, 3 * w:3 * w + 2 * RW_DECAY_LORA]
    ad = zs[:, 3 * w + 2 * RW_DECAY_LORA:3 * w + 2 * RW_DECAY_LORA + 2 * RW_ICLR_LORA]
    gd_ref[...] = zs[:, RW_IN - RW_GATE_LORA:]
    u = w0_ref[...] + _dot(jnp.tanh(wd).astype(BF16), w2_ref[...])
    a = jax.nn.sigmoid(a0_ref[...] + _dot(ad.astype(BF16), a2_ref[...]))
    lw = -jax.nn.sigmoid(u) * math.exp(-0.5)
    kkr = k * kk_ref[...]
    kkn = kkr / jnp.maximum(jnp.sqrt(_seg_sum(kkr * kkr, seg_ref[...])), 1e-12)
    r_ref[...] = r
    v_ref[...] = v
    kkn_ref[...] = kkn
    for d in range(2):
        a_d = a[:, d * w:(d + 1) * w]
        lw_ref[d] = lw[:, d * w:(d + 1) * w]
        beta_ref[d] = kkn * a_d
        kd_ref[d] = k * (1.0 + (a_d - 1.0) * ka_ref[...])


def _rw_prep(z, mu, w0, w2b, a0, a2b, k_k, k_a, seg, n_ctx):
    t = z.shape[0]
    n8 = TM // SUBLANE
    z3 = z.reshape(t // SUBLANE, SUBLANE, RW_IN)
    tok = lambda w: pl.BlockSpec((TM, w), lambda i: (i, 0))
    tok2 = pl.BlockSpec((2, TM, RW_WIDTH), lambda i: (0, i, 0))
    f = jax.ShapeDtypeStruct((t, RW_WIDTH), F32)
    f2 = jax.ShapeDtypeStruct((2, t, RW_WIDTH), F32)
    return pl.pallas_call(
        functools.partial(_rw_prep_body, n_ctx, t),
        grid=(t // TM,),
        in_specs=[tok(RW_IN),
                  pl.BlockSpec((1, SUBLANE, RW_IN), lambda i: (jnp.maximum(i * n8 - 1, 0), 0, 0)),
                  pl.BlockSpec((1, SUBLANE, RW_IN), lambda i: (jnp.minimum((i + 1) * n8, t // SUBLANE - 1), 0, 0)),
                  _full(mu.shape), _full(w0.shape), _full(w2b.shape), _full(a0.shape), _full(a2b.shape),
                  _full(k_k.shape), _full(k_a.shape), _full(seg.shape)],
        out_specs=[tok(RW_WIDTH), tok(RW_WIDTH), tok(RW_WIDTH), tok2, tok2, tok2, tok(RW_GATE_LORA)],
        out_shape=[f, f, f, f2, f2, f2, jax.ShapeDtypeStruct((t, RW_GATE_LORA), F32)],
        compiler_params=_cp(1),
        name="rw_prep",
    )(z, z3, z3, mu, w0, w2b, a0, a2b, k_k, k_a, seg)


def _rw_group_steps(probs):
    c, gl = probs[0][0].shape
    n = gl // RW_GROUP
    revs = [p[6] for p in probs]
    t_i = lax.broadcasted_iota(jnp.int32, (c, c), 0)
    s_i = lax.broadcasted_iota(jnp.int32, (c, c), 1)
    tri = {False: (s_i <= t_i).astype(BF16), True: (s_i >= t_i).astype(BF16)}
    diag_blk = (lax.broadcasted_iota(jnp.int32, (RW_GROUP * c, gl), 0) // c
                == lax.broadcasted_iota(jnp.int32, (RW_GROUP * c, gl), 1) // n)
    tt = lax.broadcasted_iota(jnp.int32, (c, RW_GROUP * c), 0)
    ss = lax.broadcasted_iota(jnp.int32, (c, RW_GROUP * c), 1) % c
    strict = {False: ss < tt, True: ss > tt}
    incl4 = {False: ss <= tt, True: ss >= tt}
    eye4 = jnp.where(ss == tt, 1.0, 0.0)

    def x4(m):
        return jnp.where(diag_blk, jnp.concatenate([m] * RW_GROUP, axis=0), jnp.zeros((), m.dtype))

    def each(fn, *lists):
        return [fn(*args) for args in zip(*lists)]

    lws = [p[3] for p in probs]
    parts = [_split3(lw) for lw in lws]
    g_inc = each(lambda rev, hml: _dot(tri[rev], hml[0]) + _dot(tri[rev], hml[1]) + _dot(tri[rev], hml[2]), revs, parts)
    e_neg = [jnp.exp(-g) for g in g_inc]
    a_t = each(lambda p, g: (-p[2] * jnp.exp(g - p[3])).astype(BF16), probs, g_inc)
    b_t = each(lambda p, e: (p[4] * e).astype(BF16), probs, e_neg)
    k_t = each(lambda p, e: (p[5] * e).astype(BF16), probs, e_neg)
    r_t = each(lambda p, g: (p[0] * jnp.exp(g)).astype(BF16), probs, g_inc)
    v_bf = [p[1].astype(BF16) for p in probs]
    decay_c = [jnp.exp(jnp.sum(lw, axis=0, keepdims=True)) for lw in lws]

    xb, xk, xv, xa = [each(x4, m) for m in (b_t, k_t, v_bf, a_t)]
    ar = each(lambda a, r: jnp.concatenate([a, r], axis=0), a_t, r_t)
    g_b = each(_dot_nt, ar, xb)
    g_k = each(_dot_nt, ar, xk)
    l_ab = each(lambda rev, g: jnp.where(strict[rev], g[:c], 0.0), revs, g_b)
    l_ak = each(lambda rev, g: jnp.where(strict[rev], g[:c], 0.0).astype(BF16), revs, g_k)
    m_rb = each(lambda rev, g: jnp.where(incl4[rev], g[c:], 0.0).astype(BF16), revs, g_b)
    m_rk = each(lambda rev, g: jnp.where(incl4[rev], g[c:], 0.0).astype(BF16), revs, g_k)
    lv = each(lambda l, x: _dot(l, x).astype(BF16), l_ak, xv)

    pw = l_ab
    tinv = [eye4 + l for l in l_ab]
    k2 = 1
    while 2 * k2 < c:
        pw = each(lambda p: _dot(p.astype(BF16), x4(p.astype(BF16))), pw)
        tinv = each(lambda t, p: t + _dot(t.astype(BF16), x4(p.astype(BF16))), tinv, pw)
        k2 *= 2
    tinv = [t.astype(BF16) for t in tinv]

    wm = each(lambda t, x: _dot(t, x).astype(BF16), tinv, xa)
    u0 = each(lambda t, l: _dot(t, x4(l)), tinv, lv)
    dcol = [jnp.concatenate([jnp.transpose(jnp.broadcast_to(d, (LANE, gl)))] * (gl // LANE), axis=1) for d in decay_c]

    def with_state(idxs, h_bf):
        pick = lambda lst: [lst[i] for i in idxs]
        u_bf = each(lambda w, h, u: (_dot(w, h) + u).astype(BF16), pick(wm), h_bf, pick(u0))
        y = each(lambda r, h, mb, u, mk, x: _dot(r, h) + _dot(mb, x4(u)) + _dot(mk, x),
                 pick(r_t), h_bf, pick(m_rb), u_bf, pick(m_rk), pick(xv))
        dh = each(lambda b, u, k, v: _dot_tn(b, u) + _dot_tn(k, v), pick(b_t), u_bf, pick(k_t), pick(v_bf))
        return list(zip(y, dh, pick(dcol)))

    return with_state


def _rw_chunk_body(rf_ref, vf_ref, kf_ref, lwf_ref, bf_ref, kdf_ref,
                   rb_ref, vb_ref, kb_ref, lwb_ref, bb_ref, kdb_ref, yf_ref, yb_ref, h_ref):
    @pl.when(pl.program_id(0) == 0)
    def _():
        h_ref[...] = jnp.zeros_like(h_ref)

    c = RW_CHUNK
    gl = RW_GROUP * RW_HEAD_DIM
    n_grp = RW_WIDTH // gl
    blk = (lax.broadcasted_iota(jnp.int32, (gl, gl), 0) // RW_HEAD_DIM
           == lax.broadcasted_iota(jnp.int32, (gl, gl), 1) // RW_HEAD_DIM)
    dirs = ((rf_ref, vf_ref, kf_ref, lwf_ref, bf_ref, kdf_ref, yf_ref, False),
            (rb_ref, vb_ref, kb_ref, lwb_ref, bb_ref, kdb_ref, yb_ref, True))
    probs, waves = [], [[] for _ in range(RW_CHUNKS_PER_STEP)]
    for d, (r_ref, v_ref, k_ref, lw_ref, b_ref, kd_ref, y_ref, rev) in enumerate(dirs):
        for w in range(RW_CHUNKS_PER_STEP):
            ci = RW_CHUNKS_PER_STEP - 1 - w if rev else w
            rows = slice(ci * c, (ci + 1) * c)
            for g in range(n_grp):
                sl = slice(g * gl, (g + 1) * gl)
                waves[w].append((len(probs), y_ref, rows, sl, d * n_grp + g))
                probs.append((r_ref[rows, sl], v_ref[rows, sl], k_ref[rows, sl], lw_ref[0][rows, sl],
                              b_ref[0][rows, sl], kd_ref[0][rows, sl], rev))
    with_state = _rw_group_steps(probs)
    for wave in waves:
        res = with_state([p for p, *_ in wave], [h_ref[idx].astype(BF16) for *_, idx in wave])
        for (_, y_ref, rows, sl, idx), (y, dh, dec) in zip(wave, res):
            y_ref[rows, sl] = y
            h_ref[idx] = (h_ref[idx] + jnp.where(blk, dh, 0.0)) * dec


def _rw_chunks(r, v, kkn, lw, beta, kd, n_ctx):
    t = r.shape[0]
    c = RW_CHUNK * RW_CHUNKS_PER_STEP
    nc = t // c
    nc_ctx = n_ctx // c

    def fwd(i):
        return i

    def bwd(i):
        return jnp.where(i < nc_ctx, nc_ctx - 1 - i, nc - 1 - (i - nc_ctx))

    def specs(m):
        one = pl.BlockSpec((c, RW_WIDTH), lambda i: (m(i), 0))
        return one

    def specs2(m, d):
        return pl.BlockSpec((1, c, RW_WIDTH), lambda i: (d, m(i), 0))

    in_specs = [specs(fwd), specs(fwd), specs(fwd), specs2(fwd, 0), specs2(fwd, 0), specs2(fwd, 0),
                specs(bwd), specs(bwd), specs(bwd), specs2(bwd, 1), specs2(bwd, 1), specs2(bwd, 1)]
    f = jax.ShapeDtypeStruct((t, RW_WIDTH), F32)
    n_state = 2 * RW_HEADS // RW_GROUP
    gl = RW_GROUP * RW_HEAD_DIM
    return pl.pallas_call(
        _rw_chunk_body,
        grid=(nc,),
        in_specs=in_specs,
        out_specs=[specs(fwd), specs(bwd)],
        out_shape=[f, f],
        scratch_shapes=[pltpu.VMEM((n_state, gl, gl), F32)],
        compiler_params=_cp(1),
        name="rw_chunks",
    )(r, v, kkn, lw, beta, kd, r, v, kkn, lw, beta, kd)


def _merge_body(n_ctx, t_off, x_ref, mod_ref, g1_ref, g2n_ref, wg_ref, yat_ref, yb_ref, yf_ref, ybw_ref, r_ref, v_ref,
                kd_ref, gd_ref, g2_ref, rk_ref, lng_ref, lnb_ref, seg_ref, woa_ref, wob_ref, wor_ref, wout_ref,
                xmid_ref, h2_ref):
    tm, d = x_ref.shape
    row0 = (pl.program_id(0) + t_off) * tm
    x = x_ref[...]
    mod = mod_ref[...]
    h = _modulated_norm(x, g1_ref[...], mod, row0, n_ctx, 1).astype(BF16)
    gates = jax.nn.sigmoid(_dot(h, wg_ref[...]))
    pa = _dot_tn(yat_ref[...].astype(BF16), woa_ref[...])
    pb = _dot(yb_ref[...].astype(BF16), wob_ref[...])
    seg = seg_ref[...]
    inv = 1.0 / RW_HEAD_DIM
    y = yf_ref[...] + ybw_ref[...]
    dy = y - _seg_sum(y, seg) * inv
    yn = dy * lax.rsqrt(_seg_sum(dy * dy, seg) * inv + RW_GN_EPS) * lng_ref[...] + lnb_ref[...]
    bonus = _seg_sum(r_ref[...] * (kd_ref[0] + kd_ref[1]) * rk_ref[...], seg) * v_ref[...]
    g = _dot(jax.nn.sigmoid(gd_ref[...]).astype(BF16), g2_ref[...])
    pr = _dot(((yn + bonus) * g).astype(BF16), wor_ref[...])
    m = gates[:, :d] * pa + gates[:, d:2 * d] * pb + gates[:, 2 * d:] * pr
    o = _dot(m.astype(BF16), wout_ref[...])
    x_mid = x + _gate_rows(mod, row0, tm, n_ctx, 1) * o
    xmid_ref[...] = x_mid
    h2_ref[...] = _modulated_norm(x_mid, g2n_ref[...], mod, row0, n_ctx, 2).astype(h2_ref.dtype)


def _merge(x_all, mod_l, g1, g2n, wg, yat, yb, yf, ybw, r, v, kd, gd, g2, rk, lng, lnb, seg, woa, wob, wor, wout,
           n_ctx, with_ctx, h2_dtype):
    t, d = x_all.shape
    t_off = 0 if with_ctx else n_ctx // TM
    n_rows = yat.shape[1]
    tok = lambda w: pl.BlockSpec((TM, w), lambda i: (i + t_off, 0))
    own = lambda w: pl.BlockSpec((TM, w), lambda i: (i, 0))
    smalls = (g2, rk, lng, lnb, seg, woa, wob, wor, wout)
    return pl.pallas_call(
        functools.partial(_merge_body, n_ctx, t_off),
        grid=(n_rows // TM,),
        in_specs=[tok(d), _full(mod_l.shape), _full(g1.shape), _full(g2n.shape), _full(wg.shape),
                  pl.BlockSpec((yat.shape[0], TM), lambda i: (0, i)), own(NA_WIDTH),
                  tok(RW_WIDTH), tok(RW_WIDTH), tok(RW_WIDTH), tok(RW_WIDTH),
                  pl.BlockSpec((2, TM, RW_WIDTH), lambda i: (0, i + t_off, 0)), tok(RW_GATE_LORA)]
                 + [_full(a.shape) for a in smalls],
        out_specs=[own(d), own(d)],
        out_shape=[jax.ShapeDtypeStruct((n_rows, d), F32), jax.ShapeDtypeStruct((n_rows, d), h2_dtype)],
        compiler_params=_cp(1),
        name="merge",
    )(x_all, mod_l, g1, g2n, wg, yat, yb, yf, ybw, r, v, kd, gd, *smalls)


def _ffn_body(n_ctx, h_ref, x_ref, mod_ref, w1_ref, w3_ref, w2_ref, o_ref):
    tm = h_ref.shape[0]
    h = h_ref[...]
    acc = jnp.zeros(o_ref.shape, F32)
    for f in range(w1_ref.shape[0]):
        a = _dot(h, w1_ref[f])
        b = _dot(h, w3_ref[f])
        acc = acc + _dot((a * jax.nn.sigmoid(a) * b).astype(BF16), w2_ref[f])
    o_ref[...] = x_ref[...] + _gate_rows(mod_ref[...], pl.program_id(0) * tm, tm, n_ctx, 2) * acc


def _ffn(h2, x_mid, mod_l, w1, w3, w2, n_ctx):
    t, d = x_mid.shape
    tok = pl.BlockSpec((TM, d), lambda i: (i, 0))
    return pl.pallas_call(
        functools.partial(_ffn_body, n_ctx),
        grid=(t // TM,),
        in_specs=[tok, tok, _full(mod_l.shape), _full(w1.shape), _full(w3.shape), _full(w2.shape)],
        out_specs=tok,
        out_shape=jax.ShapeDtypeStruct((t, d), F32),
        compiler_params=_cp(1),
        name="ffn",
    )(h2, x_mid, mod_l, w1, w3, w2)


def _router_body(h_ref, w_ref, idx_ref, gate_ref):
    logits = _dot(h_ref[...].astype(BF16), w_ref[...])
    lane_i = lax.broadcasted_iota(jnp.int32, logits.shape, 1)
    lane = lane_i.astype(F32)
    logits = jnp.where(lane_i < N_EXPERTS, logits, NEG_BIG)
    m1 = jnp.max(logits, axis=-1, keepdims=True)
    i1 = jnp.min(jnp.where(logits == m1, lane, float(LANE)), axis=-1, keepdims=True)
    rest = jnp.where(lane == i1, NEG_BIG, logits)
    m2 = jnp.max(rest, axis=-1, keepdims=True)
    i2 = jnp.min(jnp.where(rest == m2, lane, float(LANE)), axis=-1, keepdims=True)
    e2 = jnp.exp(m2 - m1)
    g1 = 1.0 / (1.0 + e2)
    lane = lane_i
    idx_ref[...] = jnp.where(lane == 0, i1, jnp.where(lane == 1, i2, 0.0)).astype(jnp.int32)
    gate_ref[...] = jnp.where(lane == 0, g1, jnp.where(lane == 1, e2 * g1, 0.0))


def _router(h2, w_router_pad):
    n, d = h2.shape
    tok = lambda w: pl.BlockSpec((TM, w), lambda i: (i, 0))
    return pl.pallas_call(
        _router_body,
        grid=(n // TM,),
        in_specs=[tok(d), _full(w_router_pad.shape)],
        out_specs=[tok(LANE), tok(LANE)],
        out_shape=[jax.ShapeDtypeStruct((n, LANE), jnp.int32), jax.ShapeDtypeStruct((n, LANE), F32)],
        compiler_params=_cp(1),
        name="router",
    )(h2, w_router_pad)


def _row_copy(src_hbm, dst_vmem, src_row, dst_row, sem):
    return pltpu.make_async_copy(src_hbm.at[pl.ds(src_row, 1)], dst_vmem.at[pl.ds(dst_row, 1)], sem)


def _gather_body(idx_hbm, src_hbm, o_ref, idx_smem, sem_idx, sem_rows):
    bm = o_ref.shape[0]
    cp = pltpu.make_async_copy(idx_hbm.at[pl.program_id(0)], idx_smem, sem_idx)
    cp.start()
    cp.wait()

    def start(g, carry):
        for u in range(GATHER_UNROLL):
            r = g * GATHER_UNROLL + u
            _row_copy(src_hbm, o_ref, idx_smem[r], r, sem_rows).start(priority=u % 2)
        return carry

    lax.fori_loop(0, bm // GATHER_UNROLL, start, 0)
    pltpu.make_async_copy(src_hbm.at[pl.ds(0, bm)], o_ref, sem_rows).wait()


def _gather_rows(idx, src, bm):
    nb = idx.shape[0]
    width = src.shape[1]
    return pl.pallas_call(
        _gather_body,
        grid=(nb,),
        in_specs=[pl.BlockSpec(memory_space=pl.ANY), pl.BlockSpec(memory_space=pl.ANY)],
        out_specs=pl.BlockSpec((bm, width), lambda b: (b, 0)),
        out_shape=jax.ShapeDtypeStruct((nb * bm, width), src.dtype),
        scratch_shapes=[pltpu.SMEM((bm,), jnp.int32), pltpu.SemaphoreType.DMA(()), pltpu.SemaphoreType.DMA(())],
        compiler_params=_cp(1),
        name="gather_rows",
    )(idx, src)


def _moe_ffn_body(be_ref, nb_ref, x_ref, w1_ref, w3_ref, w2_ref, o_ref, acc_ref):
    b, f = pl.program_id(0), pl.program_id(1)
    nf = pl.num_programs(1)

    @pl.when(f == 0)
    def _():
        acc_ref[...] = jnp.zeros_like(acc_ref)

    @pl.when(b < nb_ref[0])
    def _():
        x = x_ref[...].astype(BF16)
        a = _dot(x, w1_ref[0])
        g = _dot(x, w3_ref[0])
        acc_ref[...] += _dot((a * jax.nn.sigmoid(a) * g).astype(BF16), w2_ref[0])

    @pl.when(f == nf - 1)
    def _():
        o_ref[...] = acc_ref[...]


def _moe_ffn(blk_expert, n_used, xg, w1, w3, w2, fc):
    n_slots, d = xg.shape
    nb = n_slots // MOE_BM
    nf = w1.shape[2] // fc
    grid_spec = pltpu.PrefetchScalarGridSpec(
        num_scalar_prefetch=2,
        grid=(nb, nf),
        in_specs=[pl.BlockSpec((MOE_BM, d), lambda b, f, be, nu: (b, 0)),
                  pl.BlockSpec((1, d, fc), lambda b, f, be, nu: (be[b], 0, f)),
                  pl.BlockSpec((1, d, fc), lambda b, f, be, nu: (be[b], 0, f)),
                  pl.BlockSpec((1, fc, d), lambda b, f, be, nu: (be[b], f, 0))],
        out_specs=pl.BlockSpec((MOE_BM, d), lambda b, f, be, nu: (b, 0)),
        scratch_shapes=[pltpu.VMEM((MOE_BM, d), F32)],
    )
    return pl.pallas_call(
        _moe_ffn_body,
        grid_spec=grid_spec,
        out_shape=jax.ShapeDtypeStruct((n_slots, d), F32),
        compiler_params=_cp(2),
        name="moe_ffn",
    )(blk_expert, n_used, xg, w1, w3, w2)


def _combine_body(y0_ref, y1_ref, gate_ref, x_ref, mod_ref, o_ref):
    d = x_ref.shape[1]
    g = gate_ref[...]
    f = g[:, 0:1] * y0_ref[...] + g[:, 1:2] * y1_ref[...]
    o_ref[...] = x_ref[...] + mod_ref[0:1, 5 * d:6 * d] * f


def _moe_combine(yk, gate_pad, x_mid, mod_l):
    n, d = x_mid.shape
    nt = n // TM
    return pl.pallas_call(
        _combine_body,
        grid=(nt,),
        in_specs=[pl.BlockSpec((TM, d), lambda i: (i, 0)), pl.BlockSpec((TM, d), lambda i: (i + nt, 0)),
                  pl.BlockSpec((TM, LANE), lambda i: (i, 0)), pl.BlockSpec((TM, d), lambda i: (i, 0)),
                  _full(mod_l.shape)],
        out_specs=pl.BlockSpec((TM, d), lambda i: (i, 0)),
        out_shape=jax.ShapeDtypeStruct((n, d), F32),
        compiler_params=_cp(1),
        name="moe_combine",
    )(yk, yk, gate_pad, x_mid, mod_l)


def _moe(h2, x_mid, mod_l, router_w, w1, w3, w2, fc):
    n, d = h2.shape
    e_n = router_w.shape[1]
    idx_pad, gate_pad = _router(h2, jnp.zeros((d, LANE), BF16).at[:, :e_n].set(router_w.astype(BF16)))
    flat_e = idx_pad[:, :TOP_K].T.reshape(-1)
    onehot = (flat_e[:, None] == jnp.arange(e_n)[None, :]).astype(jnp.int32)
    rank = jnp.take_along_axis(jnp.cumsum(onehot, axis=0) - onehot, flat_e[:, None], axis=1)[:, 0]
    counts = jnp.sum(onehot, axis=0)
    padded = (counts + MOE_BM - 1) // MOE_BM * MOE_BM
    pad_ends = jnp.cumsum(padded)
    slot = ((pad_ends - padded)[flat_e] + rank).astype(jnp.int32)
    nb = TOP_K * n // MOE_BM + e_n
    n_slots = nb * MOE_BM
    tok_of_slot = jnp.zeros((n_slots,), jnp.int32).at[slot].set(jnp.tile(jnp.arange(n, dtype=jnp.int32), TOP_K))
    blk_expert = jnp.minimum(jnp.searchsorted(pad_ends, jnp.arange(nb, dtype=jnp.int32) * MOE_BM, side="right"),
                             e_n - 1).astype(jnp.int32)
    n_used = (pad_ends[-1:] // MOE_BM).astype(jnp.int32)
    xg = _gather_rows(tok_of_slot.reshape(nb, MOE_BM), h2, MOE_BM)
    yg = _moe_ffn(blk_expert, n_used, xg, w1, w3, w2, fc)
    yk = _gather_rows(slot.reshape(-1, MOE_BM), yg, MOE_BM)
    return _moe_combine(yk, gate_pad, x_mid, mod_l)


def _rope_tables(n_ctx, n_lat):
    pos = jnp.arange(n_lat)
    half = MLA_ROPE // 4
    freqs = jnp.exp(-math.log(ROPE_BASE) * jnp.arange(half, dtype=F32) / half)
    ang_r = (pos // GRID_W).astype(F32)[:, None] * freqs[None, :]
    ang_c = (pos % GRID_W).astype(F32)[:, None] * freqs[None, :]
    one = jnp.ones((n_lat, MLA_NOPE), F32)
    zero = jnp.zeros((n_lat, MLA_NOPE), F32)
    tail1 = jnp.ones((n_lat, HEAD_PAD - MLA_QK), F32)
    tail0 = jnp.zeros((n_lat, HEAD_PAD - MLA_QK), F32)
    z8 = jnp.zeros((n_lat, half), F32)
    cr, sr, cc, sc = jnp.cos(ang_r), jnp.sin(ang_r), jnp.cos(ang_c), jnp.sin(ang_c)
    cos = jnp.concatenate([one, cr, cr, cc, cc, tail1], axis=1)
    sin_a = jnp.concatenate([zero, -sr, z8, -sc, z8, tail0], axis=1)
    sin_b = jnp.concatenate([zero, z8, sr, z8, sc, tail0], axis=1)
    ctx_pad = lambda a, v: jnp.concatenate([jnp.full((n_ctx, HEAD_PAD), v, F32), a], axis=0)
    return ctx_pad(cos, 1.0), ctx_pad(sin_a, 0.0), ctx_pad(sin_b, 0.0)


def _seg_matrix(width, seg):
    i = np.arange(width)
    return jnp.asarray((i[:, None] // seg == i[None, :] // seg).astype(np.float32), dtype=BF16)


def _block_diag2(w):
    _, r, c = w.shape
    z = jnp.zeros((r, c), w.dtype)
    return jnp.concatenate([jnp.concatenate([w[0], z], axis=1), jnp.concatenate([z, w[1]], axis=1)], axis=0)


def _pad_heads(w, n_heads, width, pad_to):
    rows = w.shape[0]
    w = w.reshape(rows, n_heads, width)
    return jnp.pad(w, ((0, 0), (0, 0), (0, pad_to - width))).reshape(rows, n_heads * pad_to)


def kernel(x, c, ctx, c_ctx, mod_w, mod_b, norm1_g, norm2_g, w_in, mla_cq_g, mla_wuq, mla_ckv_g, mla_wukv, mla_qn_g, mla_kn_g, mla_wo, na_qn_g, na_kn_g, na_rpb, na_wo, rw_mu, rw_w0, rw_w2, rw_a0, rw_a2, rw_g2, rw_kk, rw_ka, rw_rk, rw_ln_g, rw_ln_b, rw_wo, w_out, ffn_w1, ffn_w3, ffn_w2, moe_router, moe_w1, moe_w3, moe_w2):
    b, n_lat, d = x.shape
    n_ctx = ctx.shape[1]
    depth = mod_w.shape[0]
    assert b == 1 and n_ctx % TM == 0 and n_lat % TM == 0 and n_lat % (NA_ROWS * GRID_W) == 0
    assert n_lat // GRID_W >= NA_KROWS and n_ctx % (RW_CHUNK * RW_CHUNKS_PER_STEP) == 0

    mod = _modulation(c, c_ctx, mod_w, mod_b)
    x_all = jnp.concatenate([ctx[0], x[0]], axis=0)
    rope_tabs = _rope_tables(n_ctx, n_lat)
    seg64 = _seg_matrix(RW_WIDTH, RW_HEAD_DIM)

    o_ckv = MLA_Q_LORA
    o_kr = o_ckv + MLA_KV_LORA
    o_na = o_kr + MLA_ROPE
    o_rz = o_na + 3 * NA_WIDTH
    o_gate = o_rz + RW_IN

    for l in range(depth):
        need_ctx = l < depth - 1
        mod_l = mod[l]
        g1 = norm1_g[l][None, :]
        g2n = norm2_g[l][None, :]
        wi = w_in[l]

        w_a = jnp.pad(wi[:, :o_na], ((0, 0), (0, 6 * LANE - o_na))).astype(BF16)
        w_na = wi[:, o_na:o_rz].astype(BF16)
        w_rz = wi[:, o_rz:o_gate].astype(BF16)
        w_g = wi[:, o_gate:].astype(BF16)
        wuq = _pad_heads(mla_wuq[l], MLA_HEADS, MLA_QK, HEAD_PAD).astype(BF16)
        wukv = mla_wukv[l].reshape(MLA_KV_LORA, MLA_HEADS, MLA_NOPE + MLA_V)
        wuk = jnp.pad(wukv[:, :, :MLA_NOPE], ((0, 0), (0, 0), (0, HEAD_PAD - MLA_NOPE))).reshape(MLA_KV_LORA, -1).astype(BF16)
        wuvt = wukv[:, :, MLA_NOPE:].reshape(MLA_KV_LORA, -1).T.astype(BF16)
        qng = jnp.pad(mla_qn_g[l], (0, HEAD_PAD - MLA_QK))[None, :]
        kng = jnp.pad(mla_kn_g[l], (0, HEAD_PAD - MLA_QK))[None, :]
        bound = 1.01 * math.sqrt(MLA_QK) * jnp.max(jnp.abs(mla_qn_g[l])) * jnp.max(jnp.abs(mla_kn_g[l]))
        shift = jnp.zeros((1, HEAD_PAD), F32).at[0, MLA_QK].set(-bound * math.log2(math.e))
        mla_wts = (w_a, mla_cq_g[l][None, :], wuq, mla_ckv_g[l][None, :], wuk, wuvt, qng, kng, shift)

        qa, ka, vta = _mla_proj(x_all, mod_l, g1, mla_wts, rope_tabs, n_ctx)
        nt_ctx, nt_all = n_ctx // TM, (n_ctx + n_lat) // TM

        def mla_attention(fast):
            def run(qa, ka, vta):
                y = _mla_attn(qa, ka, vta, nt_ctx, nt_all - nt_ctx, nt_all, fast)
                if need_ctx:
                    y = jnp.concatenate([_mla_attn(qa, ka, vta, 0, nt_ctx, nt_ctx, fast), y], axis=1)
                return y
            return run

        yat = lax.cond(bound <= MLA_FAST_BOUND, mla_attention(True), mla_attention(False), qa, ka, vta)

        qb, kb, vb = _na_proj(x_all, mod_l, g1, w_na, jnp.tile(na_qn_g[l], NA_HEADS)[None, :],
                              jnp.tile(na_kn_g[l], NA_HEADS)[None, :], seg64, n_ctx)
        yb = _na_attn(qb, kb, vb, _na_bias_tables(na_rpb[l]), n_ctx)
        if need_ctx:
            yb = jnp.concatenate([_ctx_attn(qb, kb, vb, n_ctx), yb], axis=0)

        z = _rz_proj(x_all, mod_l, g1, w_rz, n_ctx)
        r, v, kkn, lw, beta, kd, gd = _rw_prep(
            z, rw_mu[l], rw_w0[l].reshape(1, -1), _block_diag2(rw_w2[l]).astype(BF16), rw_a0[l].reshape(1, -1),
            _block_diag2(rw_a2[l]).astype(BF16), rw_kk[l][None, :], rw_ka[l][None, :], seg64, n_ctx)
        yf, ybw = _rw_chunks(r, v, kkn, lw, beta, kd, n_ctx)

        x_mid, h2 = _merge(x_all, mod_l, g1, g2n, w_g, yat, yb, yf, ybw, r, v, kd, gd,
                           rw_g2[l].astype(BF16), rw_rk[l].reshape(1, -1), rw_ln_g[l][None, :], rw_ln_b[l][None, :], seg64,
                           mla_wo[l].astype(BF16), na_wo[l].astype(BF16), rw_wo[l].astype(BF16), w_out[l].astype(BF16),
                           n_ctx, need_ctx, BF16 if l % 2 == 0 else F32)

        if l % 2 == 0:
            fc = FFN_FC
            w1 = ffn_w1[l // 2]
            nf = w1.shape[1] // fc
            w1r = w1.reshape(d, nf, fc).transpose(1, 0, 2).astype(BF16)
            w3r = ffn_w3[l // 2].reshape(d, nf, fc).transpose(1, 0, 2).astype(BF16)
            w2r = ffn_w2[l // 2].reshape(nf, fc, d).astype(BF16)
            if need_ctx:
                x_all = _ffn(h2, x_mid, mod_l, w1r, w3r, w2r, n_ctx)
            else:
                x_lat = _ffn(h2, x_mid, mod_l, w1r, w3r, w2r, 0)
        else:
            moe_args = (moe_router[l // 2], moe_w1[l // 2].astype(BF16), moe_w3[l // 2].astype(BF16),
                        moe_w2[l // 2].astype(BF16), MOE_FC)
            if need_ctx:
                lat = _moe(h2[n_ctx:], x_mid[n_ctx:], mod_l, *moe_args)
                ctx_rows = _moe(h2[:n_ctx], x_mid[:n_ctx], mod_l.at[0].set(mod_l[1]), *moe_args)
                x_all = jnp.concatenate([ctx_rows, lat], axis=0)
            else:
                x_lat = _moe(h2, x_mid, mod_l, *moe_args)
    return x_lat[None]
```

```python
import functools
import math

import numpy as np
import jax
import jax.numpy as jnp
from jax import lax
from jax.experimental import pallas as pl
from jax.experimental.pallas import tpu as pltpu

F32 = jnp.float32
BF16 = jnp.bfloat16

GRID_W = 64
MLA_HEADS, MLA_NOPE, MLA_ROPE, MLA_V = 8, 64, 32, 64
MLA_QK = MLA_NOPE + MLA_ROPE
MLA_Q_LORA, MLA_KV_LORA = 384, 256
NA_HEADS, NA_HEAD_DIM = 8, 64
NA_WIDTH = NA_HEADS * NA_HEAD_DIM
NA_WIN_ROWS, NA_WIN_COLS = 8, 16
RW_HEADS, RW_HEAD_DIM = 8, 64
RW_WIDTH = RW_HEADS * RW_HEAD_DIM
RW_DECAY_LORA, RW_ICLR_LORA, RW_GATE_LORA = 64, 64, 128
RW_GN_EPS = 64e-5
RW_IN = 3 * RW_WIDTH + 2 * RW_DECAY_LORA + 2 * RW_ICLR_LORA + RW_GATE_LORA
N_EXPERTS, TOP_K = 8, 2
ROPE_BASE = 10000.0
NORM_EPS = 1e-6

LANE = 128
SUBLANE = 8
VMEM_LIMIT = 56 * 1024 * 1024

TM = 256
PROJ_ROW_GROUPS = 2
HEAD_PAD = LANE
MLA_Q_STREAMS = 2
MLA_FAST_UNROLL = 65
MLA_KEY_TILES_PER_STEP = 5
MLA_FAST_BOUND = 40.0
RW_CHUNK = 64
RW_GROUP = 4
RW_CHUNKS_PER_STEP = 2
NA_ROWS = 4
NA_KROWS = NA_ROWS + NA_WIN_ROWS
MOE_BM = 512
MOE_FC = 1792
GATHER_UNROLL = 64
FFN_FC = 1408
NEG_BIG = -1e30


def _cp(n_axes, vmem=VMEM_LIMIT):
    return pltpu.CompilerParams(dimension_semantics=("arbitrary",) * n_axes, vmem_limit_bytes=vmem)


def _dot(a, b):
    return jnp.dot(a, b, preferred_element_type=F32)


def _dot_nt(a, b):
    return lax.dot_general(a, b, (((1,), (1,)), ((), ())), preferred_element_type=F32)


def _dot_tn(a, b):
    return lax.dot_general(a, b, (((0,), (0,)), ((), ())), preferred_element_type=F32)


def _split2(x):
    hi = x.astype(BF16)
    lo = (x - hi.astype(F32)).astype(BF16)
    return hi, lo


def _split3(x):
    hi = x.astype(BF16)
    r1 = x - hi.astype(F32)
    mid = r1.astype(BF16)
    lo = (r1 - mid.astype(F32)).astype(BF16)
    return hi, mid, lo


def _seg_sum(x, seg01):
    hi, lo = _split2(x)
    return _dot(hi, seg01) + _dot(lo, seg01)


def _full(shape):
    nd = len(shape)
    return pl.BlockSpec(shape, lambda *_: (0,) * nd)


def _modulated_norm(x, g, mod, first_row, n_ctx, which):
    d = x.shape[1]
    off = 0 if which == 1 else 3 * d
    y = x * lax.rsqrt(jnp.mean(x * x, axis=-1, keepdims=True) + NORM_EPS) * g
    rows = first_row + lax.broadcasted_iota(jnp.int32, (x.shape[0], 1), 0)
    is_ctx = rows < n_ctx
    sh = jnp.where(is_ctx, mod[1:2, off:off + d], mod[0:1, off:off + d])
    sc = jnp.where(is_ctx, mod[1:2, off + d:off + 2 * d], mod[0:1, off + d:off + 2 * d])
    return y * (1.0 + sc) + sh


def _gate_rows(mod, first_row, n_rows, n_ctx, which):
    d = mod.shape[1] // 6
    off = 2 * d if which == 1 else 5 * d
    rows = first_row + lax.broadcasted_iota(jnp.int32, (n_rows, 1), 0)
    return jnp.where(rows < n_ctx, mod[1:2, off:off + d], mod[0:1, off:off + d])


def _mod_body(cc_ref, w_ref, b_ref, o_ref):
    cc = cc_ref[...]
    s = cc * jax.nn.sigmoid(cc)
    o_ref[0] = _dot(s.astype(BF16), w_ref[0].astype(BF16)) + b_ref[0]


def _modulation(c, c_ctx, mod_w, mod_b):
    depth, d, d6 = mod_w.shape
    nt = d6 // 4
    cc = jnp.zeros((SUBLANE, d), F32).at[0].set(c[0]).at[1].set(c_ctx)
    return pl.pallas_call(
        _mod_body,
        grid=(depth, d6 // nt),
        in_specs=[_full((SUBLANE, d)),
                  pl.BlockSpec((1, d, nt), lambda l, j: (l, 0, j)),
                  pl.BlockSpec((1, 1, nt), lambda l, j: (l, 0, j))],
        out_specs=pl.BlockSpec((1, SUBLANE, nt), lambda l, j: (l, 0, j)),
        out_shape=jax.ShapeDtypeStruct((depth, SUBLANE, d6), F32),
        compiler_params=_cp(2),
        name="modulation",
    )(cc, mod_w, mod_b.reshape(depth, 1, d6))


def _rms(x, width):
    return x * lax.rsqrt(jnp.sum(x * x, axis=-1, keepdims=True) * (1.0 / width) + NORM_EPS)


def _mla_proj_rows(n_ctx, r0, nr, x_ref, mod_ref, g1_ref, wa_ref, cqg_ref, wuq_ref, ckvg_ref, wuk_ref, wuvt_ref,
                   qng_ref, kng_ref, shift_ref, rope_ref, q_ref, k_ref, vt_ref):
    rs = slice(r0, r0 + nr)
    row0 = pl.program_id(0) * x_ref.shape[0] + r0
    h = _modulated_norm(x_ref[rs, :], g1_ref[...], mod_ref[...], row0, n_ctx, 1).astype(BF16)
    yield
    z = _dot(h, wa_ref[...])
    cq = z[:, :MLA_Q_LORA]
    ckv = z[:, MLA_Q_LORA:MLA_Q_LORA + MLA_KV_LORA]
    kr = z[:, MLA_Q_LORA + MLA_KV_LORA:]
    cqn = (_rms(cq, MLA_Q_LORA) * cqg_ref[...]).astype(BF16)
    ckvn = (_rms(ckv, MLA_KV_LORA) * ckvg_ref[...]).astype(BF16)
    yield
    q = _dot(cqn, wuq_ref[...])
    kn = _dot(ckvn, wuk_ref[...])
    vt_ref[0, :, rs] = _dot_nt(wuvt_ref[...], ckvn).astype(BF16)
    rows = row0 + lax.broadcasted_iota(jnp.int32, (nr, 1), 0)
    pos = jnp.maximum(rows - n_ctx, 0)
    ang = ((pos // GRID_W).astype(F32) * rope_ref[0:1] + (pos % GRID_W).astype(F32) * rope_ref[1:2])
    ang = jnp.where(rows >= n_ctx, ang, 0.0)
    cos = jnp.cos(ang)
    sin_s = jnp.sin(ang) * rope_ref[2:3]
    yield

    def rope(x):
        return x * cos + pltpu.roll(x, HEAD_PAD // 2, 1) * sin_s

    q_scale = MLA_QK ** -0.5 * math.log2(math.e)
    one_lane = jnp.where(lax.broadcasted_iota(jnp.int32, (1, HEAD_PAD), 1) == MLA_QK, 1.0, 0.0)
    sls = [slice(hh * HEAD_PAD, (hh + 1) * HEAD_PAD) for hh in range(MLA_HEADS)]
    qn = [_rms(q[:, sl], MLA_QK) * qng_ref[...] for sl in sls]
    kk = [_rms(kn[:, sl] + kr, MLA_QK) * kng_ref[...] for sl in sls]
    yield
    qr = [rope(x) * q_scale + shift_ref[...] for x in qn]
    kr_ = [rope(x) + one_lane for x in kk]
    yield
    for hh, sl in enumerate(sls):
        q_ref[rs, sl] = qr[hh].astype(BF16)
        k_ref[rs, sl] = kr_[hh].astype(BF16)


def _round_robin(gens):
    while gens:
        gens = [g for g in gens if next(g, StopIteration) is not StopIteration]


def _mla_proj_body(n_ctx, *refs):
    tm = refs[0].shape[0]
    nr = tm // PROJ_ROW_GROUPS
    _round_robin([_mla_proj_rows(n_ctx, g * nr, nr, *refs) for g in range(PROJ_ROW_GROUPS)])


def _mla_proj(x_all, mod_l, g1, wts, n_ctx):
    t, d = x_all.shape
    nt = t // TM
    hp = MLA_HEADS * HEAD_PAD
    tok = lambda w: pl.BlockSpec((TM, w), lambda i: (i, 0))
    return pl.pallas_call(
        functools.partial(_mla_proj_body, n_ctx),
        grid=(nt,),
        in_specs=[tok(d), _full(mod_l.shape), _full(g1.shape)] + [_full(w.shape) for w in wts],
        out_specs=[tok(hp), tok(hp), pl.BlockSpec((1, MLA_HEADS * MLA_V, TM), lambda i: (i, 0, 0))],
        out_shape=[jax.ShapeDtypeStruct((t, hp), BF16), jax.ShapeDtypeStruct((t, hp), BF16),
                   jax.ShapeDtypeStruct((nt, MLA_HEADS * MLA_V, TM), BF16)],
        compiler_params=_cp(1),
        name="mla_proj",
    )(x_all, mod_l, g1, *wts)


def _mla_attn_fast_body(unroll, n_streams, *refs):
    q_refs, (k_ref, vt_ref, o_ref) = refs[:n_streams], refs[n_streams:]
    tq = q_refs[0].shape[0]
    nk, _, tk = vt_ref.shape
    qts = [q_ref[...].astype(F32).T.astype(BF16) for q_ref in q_refs]

    def scores(j):
        j = jnp.minimum(j, nk - 1)
        kj = k_ref[pl.ds(pl.multiple_of(j * tk, tk), tk), :]
        return tuple(_dot(kj, qt) for qt in qts)

    def probs(ss, ls):
        ps = tuple(jnp.exp2(s) for s in ss)
        ls = tuple(l + jnp.sum(p.reshape(tk // SUBLANE, SUBLANE, tq), axis=0) for l, p in zip(ls, ps))
        return tuple(p.astype(BF16) for p in ps), ls

    def step(it, carry):
        accs, ls, ss, ps = carry
        for u in range(unroll):
            j = it * unroll + u
            s_new = scores(j + 2)
            p_new, ls_new = probs(ss, ls)
            ls = tuple(jnp.where(j + 1 < nk, ln, l) for ln, l in zip(ls_new, ls))
            vj = vt_ref[j]
            accs = tuple(a + _dot(vj, p) for a, p in zip(accs, ps))
            ss, ps = s_new, p_new
        return accs, ls, ss, ps

    zeros_l = tuple(jnp.zeros((SUBLANE, tq), F32) for _ in range(n_streams))
    p0, l0 = probs(scores(0), zeros_l)
    init = (tuple(jnp.zeros((MLA_V, tq), F32) for _ in range(n_streams)), l0, scores(1), p0)
    accs, ls, _, _ = lax.fori_loop(0, nk // unroll, step, init)
    for i in range(n_streams):
        o_ref[:, i * tq:(i + 1) * tq] = accs[i] / jnp.sum(ls[i], axis=0, keepdims=True)


def _mla_attn_online_body(kb, q_ref, k_ref, vt_ref, o_ref):
    tq = q_ref.shape[0]
    nk, _, tk = vt_ref.shape
    nblk = nk // kb
    rows = kb * tk
    q = q_ref[...]

    def scores(j):
        return _dot_nt(k_ref[pl.ds(pl.multiple_of(j * rows, rows), rows), :], q)

    def consume(j, m, l, acc, s):
        m_new = jnp.maximum(m, jnp.max(s, axis=0, keepdims=True))
        alpha = jnp.exp2(m - m_new)
        p = jnp.exp2(s - m_new)
        l = alpha * l + jnp.sum(p, axis=0, keepdims=True)
        pb = p.astype(BF16)
        pv = _dot(vt_ref[j * kb], pb[0:tk])
        for i in range(1, kb):
            pv = pv + _dot(vt_ref[j * kb + i], pb[i * tk:(i + 1) * tk])
        return m_new, l, alpha * acc + pv

    def step(j, carry):
        m, l, acc, s = carry
        s_next = scores(j + 1)
        m, l, acc = consume(j, m, l, acc, s)
        return m, l, acc, s_next

    init = (jnp.full((1, tq), NEG_BIG, F32), jnp.zeros((1, tq), F32), jnp.zeros((MLA_V, tq), F32), scores(0))
    m, l, acc, s = lax.fori_loop(0, nblk - 1, step, init)
    _, l, acc = consume(nblk - 1, m, l, acc, s)
    o_ref[...] = acc / l


def _largest_divisor(n, cap):
    return max(b for b in range(1, cap + 1) if n % b == 0)


def _mla_attn(q, k, vt, q_tile0, nq, nk, fast):
    if fast:
        ns = MLA_Q_STREAMS if nq % MLA_Q_STREAMS == 0 else 1
        body = functools.partial(_mla_attn_fast_body, _largest_divisor(nk, MLA_FAST_UNROLL), ns)
    else:
        ns = 1
        body = functools.partial(_mla_attn_online_body, _largest_divisor(nk, MLA_KEY_TILES_PER_STEP))
    q_specs = [pl.BlockSpec((TM, HEAD_PAD), lambda h, i, s=s: (ns * i + q_tile0 + s, h)) for s in range(ns)]
    return pl.pallas_call(
        body,
        grid=(MLA_HEADS, nq // ns),
        in_specs=q_specs + [pl.BlockSpec((nk * TM, HEAD_PAD), lambda h, i: (0, h)),
                            pl.BlockSpec((nk, MLA_V, TM), lambda h, i: (0, h, 0))],
        out_specs=pl.BlockSpec((MLA_V, ns * TM), lambda h, i: (h, i)),
        out_shape=jax.ShapeDtypeStruct((MLA_HEADS * MLA_V, nq * TM), F32),
        compiler_params=_cp(2),
        name="mla_attn_fast" if fast else "mla_attn_online",
    )(*([q] * ns), k, vt)


def _na_proj_body(n_ctx, x_ref, mod_ref, g1_ref, w_ref, qg_ref, kg_ref, seg_ref, q_ref, k_ref, v_ref):
    tm = x_ref.shape[0]
    h = _modulated_norm(x_ref[...], g1_ref[...], mod_ref[...], pl.program_id(0) * tm, n_ctx, 1).astype(BF16)
    z = _dot(h, w_ref[...])
    seg = seg_ref[...]
    inv = 1.0 / NA_HEAD_DIM

    def head_norm(y, g):
        return y * lax.rsqrt(_seg_sum(y * y, seg) * inv + NORM_EPS) * g

    q_ref[...] = (head_norm(z[:, :NA_WIDTH], qg_ref[...]) * (NA_HEAD_DIM ** -0.5)).astype(BF16)
    k_ref[...] = head_norm(z[:, NA_WIDTH:2 * NA_WIDTH], kg_ref[...]).astype(BF16)
    v_ref[...] = z[:, 2 * NA_WIDTH:].astype(BF16)


def _na_proj(x_all, mod_l, g1, w_na, qg, kg, seg, n_ctx):
    t, d = x_all.shape
    tok = lambda w: pl.BlockSpec((TM, w), lambda i: (i, 0))
    return pl.pallas_call(
        functools.partial(_na_proj_body, n_ctx),
        grid=(t // TM,),
        in_specs=[tok(d), _full(mod_l.shape), _full(g1.shape), _full(w_na.shape), _full(qg.shape), _full(kg.shape),
                  _full(seg.shape)],
        out_specs=[tok(NA_WIDTH)] * 3,
        out_shape=[jax.ShapeDtypeStruct((t, NA_WIDTH), BF16)] * 3,
        compiler_params=_cp(1),
        name="na_proj",
    )(x_all, mod_l, g1, w_na, qg, kg, seg)


def _head_masks():
    lane = lax.broadcasted_iota(jnp.int32, (1, LANE), 1)
    return [(lane // NA_HEAD_DIM == h) for h in range(LANE // NA_HEAD_DIM)]


def _na_attn_body(n_ctx, n_grid_rows, q_ref, k_ref, v_ref, bias_ref, o_ref):
    nq = q_ref.shape[0]
    nkl = NA_KROWS * GRID_W
    g = pl.program_id(1)
    kr0 = jnp.clip(g * NA_ROWS - NA_WIN_ROWS // 2, 0, n_grid_rows - NA_KROWS)
    start = pl.multiple_of(n_ctx + kr0 * GRID_W, GRID_W)
    k_loc = k_ref[pl.ds(start, nkl), :]
    v_loc = v_ref[pl.ds(start, nkl), :]
    k_ctx = k_ref[0:n_ctx, :]
    v_ctx = v_ref[0:n_ctx, :]
    q = q_ref[...]
    out = jnp.zeros((nq, LANE), F32)
    for h, hm in enumerate(_head_masks()):
        qh = jnp.where(hm, q, jnp.zeros_like(q))
        s_loc = _dot_nt(qh, k_loc) + bias_ref[0, h]
        s_ctx = _dot_nt(qh, k_ctx)
        m = jnp.maximum(jnp.max(s_loc, axis=-1, keepdims=True), jnp.max(s_ctx, axis=-1, keepdims=True))
        p_loc = jnp.exp(s_loc - m)
        p_ctx = jnp.exp(s_ctx - m)
        l = jnp.sum(p_loc, axis=-1, keepdims=True) + jnp.sum(p_ctx, axis=-1, keepdims=True)
        o = (_dot(p_loc.astype(BF16), v_loc) + _dot(p_ctx.astype(BF16), v_ctx)) / l
        out = jnp.where(hm, o, out)
    o_ref[...] = out


def _na_attn(q, k, v, bias, n_ctx):
    t = q.shape[0]
    n_lat = t - n_ctx
    n_grid_rows = n_lat // GRID_W
    nq = NA_ROWS * GRID_W
    ng = n_grid_rows // NA_ROWS
    q_blk0 = n_ctx // nq
    npairs = NA_WIDTH // LANE

    def case(g):
        return jnp.where(g == 0, 0, jnp.where(g == ng - 1, 2, 1))

    return pl.pallas_call(
        functools.partial(_na_attn_body, n_ctx, n_grid_rows),
        grid=(npairs, ng),
        in_specs=[pl.BlockSpec((nq, LANE), lambda p, g: (g + q_blk0, p)),
                  pl.BlockSpec((t, LANE), lambda p, g: (0, p)),
                  pl.BlockSpec((t, LANE), lambda p, g: (0, p)),
                  pl.BlockSpec((1, LANE // NA_HEAD_DIM, nq, NA_KROWS * GRID_W), lambda p, g: (case(g), p, 0, 0))],
        out_specs=pl.BlockSpec((nq, LANE), lambda p, g: (g, p)),
        out_shape=jax.ShapeDtypeStruct((n_lat, NA_WIDTH), F32),
        compiler_params=_cp(2),
        name="na_attn",
    )(q, k, v, bias)


def _ctx_attn_body(q_ref, k_ref, v_ref, o_ref):
    q, k, v = q_ref[...], k_ref[...], v_ref[...]
    out = jnp.zeros(o_ref.shape, F32)
    for hm in _head_masks():
        s = _dot_nt(jnp.where(hm, q, jnp.zeros_like(q)), k)
        p = jnp.exp(s - jnp.max(s, axis=-1, keepdims=True))
        o = _dot(p.astype(BF16), v) / jnp.sum(p, axis=-1, keepdims=True)
        out = jnp.where(hm, o, out)
    o_ref[...] = out


def _ctx_attn(q, k, v, n_ctx):
    npairs = NA_WIDTH // LANE
    blk = pl.BlockSpec((n_ctx, LANE), lambda p: (0, p))
    return pl.pallas_call(
        _ctx_attn_body,
        grid=(npairs,),
        in_specs=[blk, blk, blk],
        out_specs=blk,
        out_shape=jax.ShapeDtypeStruct((n_ctx, NA_WIDTH), F32),
        compiler_params=_cp(1),
        name="ctx_attn",
    )(q, k, v)


def _na_bias_tables(rpb):
    h = rpb.shape[0]
    qc = np.arange(GRID_W)[:, None]
    kc = np.arange(GRID_W)[None, :]
    col_start = np.clip(qc - NA_WIN_COLS // 2, 0, GRID_W - NA_WIN_COLS)
    col_ok = (kc >= col_start) & (kc < col_start + NA_WIN_COLS)
    sel = (kc - qc + NA_WIN_COLS - 1)[None] == np.arange(2 * NA_WIN_COLS - 1)[:, None, None]
    by_row = jnp.einsum("hrd,dqk->hrqk", rpb.astype(F32), jnp.asarray(sel & col_ok[None], F32),
                        precision=lax.Precision.HIGHEST)
    by_row = jnp.where(col_ok[None, None], by_row, NEG_BIG)
    masked = jnp.full((h, GRID_W, GRID_W), NEG_BIG, F32)
    tabs = []
    for off, first_row in ((0, lambda qr: 0), (NA_WIN_ROWS // 2, lambda qr: qr), (NA_WIN_ROWS, lambda qr: NA_ROWS)):
        q_rows = []
        for qr in range(NA_ROWS):
            lo = first_row(qr)
            blocks = [by_row[:, kr - off - qr + NA_WIN_ROWS - 1] if lo <= kr < lo + NA_WIN_ROWS else masked
                      for kr in range(NA_KROWS)]
            q_rows.append(jnp.concatenate(blocks, axis=2))
        tabs.append(jnp.concatenate(q_rows, axis=1))
    return jnp.stack(tabs)


def _rw_prep_body(n_ctx, n_tok, x_ref, xp_ref, xn_ref, mod_ref, g1_ref, wrz_ref, mu_ref, w0_ref, w2_ref, a0_ref, a2_ref,
                  kk_ref, ka_ref, seg_ref, r_ref, v_ref, kkn_ref, lw_ref, beta_ref, kd_ref, gd_ref):
    tm = x_ref.shape[0]
    row0 = pl.program_id(0) * tm
    x_ext = jnp.concatenate([xp_ref[0], x_ref[...], xn_ref[0]], axis=0)
    h = _modulated_norm(x_ext, g1_ref[...], mod_ref[...], row0 - SUBLANE, n_ctx, 1).astype(BF16)
    z_ext = _dot(h, wrz_ref[...])
    z = z_ext[SUBLANE:SUBLANE + tm]
    loc = lax.broadcasted_iota(jnp.int32, (tm, 1), 0)
    rows = row0 + loc
    zp = jnp.where(loc == 0, z_ext[SUBLANE - 1:SUBLANE, :], pltpu.roll(z, 1, 0))
    zp = jnp.where((rows == 0) | (rows == n_ctx), 0.0, zp)
    zn = jnp.where(loc == tm - 1, z_ext[SUBLANE + tm:SUBLANE + tm + 1, :], pltpu.roll(z, tm - 1, 0))
    zn = jnp.where((rows == n_ctx - 1) | (rows == n_tok - 1), 0.0, zn)
    mu = mu_ref[...]
    zs = z + mu[0:1] * (zp - z) + mu[1:2] * (zn - z)
    w = RW_WIDTH
    r, k, v = zs[:, :w], zs[:, w:2 * w], zs[:, 2 * w:3 * w]
    wd = zs[:, 3 * w:3 * w + 2 * RW_DECAY_LORA]
    ad = zs[:, 3 * w + 2 * RW_DECAY_LORA:3 * w + 2 * RW_DECAY_LORA + 2 * RW_ICLR_LORA]
    gd_ref[...] = zs[:, RW_IN - RW_GATE_LORA:]
    u = w0_ref[...] + _dot(jnp.tanh(wd).astype(BF16), w2_ref[...])
    a = jax.nn.sigmoid(a0_ref[...] + _dot(ad.astype(BF16), a2_ref[...]))
    lw = -jax.nn.sigmoid(u) * math.exp(-0.5)
    kkr = k * kk_ref[...]
    kkn = kkr / jnp.maximum(jnp.sqrt(_seg_sum(kkr * kkr, seg_ref[...])), 1e-12)
    r_ref[...] = r
    v_ref[...] = v
    kkn_ref[...] = kkn
    for d in range(2):
        a_d = a[:, d * w:(d + 1) * w]
        lw_ref[d] = lw[:, d * w:(d + 1) * w]
        beta_ref[d] = kkn * a_d
        kd_ref[d] = k * (1.0 + (a_d - 1.0) * ka_ref[...])


def _rw_prep(x_all, mod_l, g1, w_rz, mu, w0, w2b, a0, a2b, k_k, k_a, seg, n_ctx):
    t, d = x_all.shape
    n8 = TM // SUBLANE
    x3 = x_all.reshape(t // SUBLANE, SUBLANE, d)
    tok = lambda w: pl.BlockSpec((TM, w), lambda i: (i, 0))
    tok2 = pl.BlockSpec((2, TM, RW_WIDTH), lambda i: (0, i, 0))
    f = jax.ShapeDtypeStruct((t, RW_WIDTH), F32)
    f2 = jax.ShapeDtypeStruct((2, t, RW_WIDTH), F32)
    smalls = (mod_l, g1, w_rz, mu, w0, w2b, a0, a2b, k_k, k_a, seg)
    return pl.pallas_call(
        functools.partial(_rw_prep_body, n_ctx, t),
        grid=(t // TM,),
        in_specs=[tok(d),
                  pl.BlockSpec((1, SUBLANE, d), lambda i: (jnp.maximum(i * n8 - 1, 0), 0, 0)),
                  pl.BlockSpec((1, SUBLANE, d), lambda i: (jnp.minimum((i + 1) * n8, t // SUBLANE - 1), 0, 0))]
                 + [_full(a.shape) for a in smalls],
        out_specs=[tok(RW_WIDTH), tok(RW_WIDTH), tok(RW_WIDTH), tok2, tok2, tok2, tok(RW_GATE_LORA)],
        out_shape=[f, f, f, f2, f2, f2, jax.ShapeDtypeStruct((t, RW_GATE_LORA), F32)],
        compiler_params=_cp(1),
        name="rw_prep",
    )(x_all, x3, x3, *smalls)


def _rw_group_steps(probs):
    c, gl = probs[0][0].shape
    n = gl // RW_GROUP
    revs = [p[6] for p in probs]
    t_i = lax.broadcasted_iota(jnp.int32, (c, c), 0)
    s_i = lax.broadcasted_iota(jnp.int32, (c, c), 1)
    tri = {False: (s_i <= t_i).astype(BF16), True: (s_i >= t_i).astype(BF16)}
    diag_blk = (lax.broadcasted_iota(jnp.int32, (RW_GROUP * c, gl), 0) // c
                == lax.broadcasted_iota(jnp.int32, (RW_GROUP * c, gl), 1) // n)
    tt = lax.broadcasted_iota(jnp.int32, (c, RW_GROUP * c), 0)
    ss = lax.broadcasted_iota(jnp.int32, (c, RW_GROUP * c), 1) % c
    strict = {False: ss < tt, True: ss > tt}
    incl4 = {False: ss <= tt, True: ss >= tt}
    eye4 = jnp.where(ss == tt, 1.0, 0.0)

    def x4(m):
        return jnp.where(diag_blk, jnp.concatenate([m] * RW_GROUP, axis=0), jnp.zeros((), m.dtype))

    def each(fn, *lists):
        return [fn(*args) for args in zip(*lists)]

    lws = [p[3] for p in probs]
    parts = [_split3(lw) for lw in lws]
    g_inc = each(lambda rev, hml: _dot(tri[rev], hml[0]) + _dot(tri[rev], hml[1]) + _dot(tri[rev], hml[2]), revs, parts)
    e_neg = [jnp.exp(-g) for g in g_inc]
    a_t = each(lambda p, g: (-p[2] * jnp.exp(g - p[3])).astype(BF16), probs, g_inc)
    b_t = each(lambda p, e: (p[4] * e).astype(BF16), probs, e_neg)
    k_t = each(lambda p, e: (p[5] * e).astype(BF16), probs, e_neg)
    r_t = each(lambda p, g: (p[0] * jnp.exp(g)).astype(BF16), probs, g_inc)
    v_bf = [p[1].astype(BF16) for p in probs]
    decay_c = [jnp.exp(jnp.sum(lw, axis=0, keepdims=True)) for lw in lws]

    xb, xk, xv, xa = [each(x4, m) for m in (b_t, k_t, v_bf, a_t)]
    ar = each(lambda a, r: jnp.concatenate([a, r], axis=0), a_t, r_t)
    g_b = each(_dot_nt, ar, xb)
    g_k = each(_dot_nt, ar, xk)
    l_ab = each(lambda rev, g: jnp.where(strict[rev], g[:c], 0.0), revs, g_b)
    l_ak = each(lambda rev, g: jnp.where(strict[rev], g[:c], 0.0).astype(BF16), revs, g_k)
    m_rb = each(lambda rev, g: jnp.where(incl4[rev], g[c:], 0.0).astype(BF16), revs, g_b)
    m_rk = each(lambda rev, g: jnp.where(incl4[rev], g[c:], 0.0).astype(BF16), revs, g_k)
    lm_v = each(lambda l, m, x: _dot(jnp.concatenate([l, m], axis=0), x), l_ak, m_rk, xv)
    lv = [t[:c].astype(BF16) for t in lm_v]
    y_v = [t[c:] for t in lm_v]

    pw = [l.astype(BF16) for l in l_ab]
    tinv = [eye4 + l for l in l_ab]
    pw = each(lambda p: _dot(p, x4(p)).astype(BF16), pw)
    k2 = 2
    while 2 * k2 < c:
        tp = each(lambda t, p: _dot(jnp.concatenate([t.astype(BF16), p], axis=0), x4(p)), tinv, pw)
        tinv = each(lambda t, r: t + r[:c], tinv, tp)
        pw = [r[c:].astype(BF16) for r in tp]
        k2 *= 2
    tinv = each(lambda t, p: (t + _dot(t.astype(BF16), x4(p))).astype(BF16), tinv, pw)

    wm = each(lambda t, x: _dot(t, x).astype(BF16), tinv, xa)
    u0 = each(lambda t, l: _dot(t, x4(l)), tinv, lv)
    wr = each(lambda w, r: jnp.concatenate([w, r], axis=0), wm, r_t)
    bk = each(lambda b, k: jnp.concatenate([b, k], axis=0), b_t, k_t)
    dcol = [jnp.concatenate([jnp.transpose(jnp.broadcast_to(d, (LANE, gl)))] * (gl // LANE), axis=1) for d in decay_c]

    def with_state(idxs, h_bf):
        pick = lambda lst: [lst[i] for i in idxs]
        wr_h = each(_dot, pick(wr), h_bf)
        u_bf = each(lambda t, u: (t[:c] + u).astype(BF16), wr_h, pick(u0))
        y = each(lambda t, mb, u, yv: t[c:] + _dot(mb, x4(u)) + yv, wr_h, pick(m_rb), u_bf, pick(y_v))
        dh = each(lambda b, u, v: _dot_tn(b, jnp.concatenate([u, v], axis=0)), pick(bk), u_bf, pick(v_bf))
        return list(zip(y, dh, pick(dcol)))

    return with_state


def _rw_chunk_body(rf_ref, vf_ref, kf_ref, lwf_ref, bf_ref, kdf_ref,
                   rb_ref, vb_ref, kb_ref, lwb_ref, bb_ref, kdb_ref, yf_ref, yb_ref, h_ref):
    @pl.when(pl.program_id(0) == 0)
    def _():
        h_ref[...] = jnp.zeros_like(h_ref)

    c = RW_CHUNK
    gl = RW_GROUP * RW_HEAD_DIM
    n_grp = RW_WIDTH // gl
    blk = (lax.broadcasted_iota(jnp.int32, (gl, gl), 0) // RW_HEAD_DIM
           == lax.broadcasted_iota(jnp.int32, (gl, gl), 1) // RW_HEAD_DIM)
    dirs = ((rf_ref, vf_ref, kf_ref, lwf_ref, bf_ref, kdf_ref, yf_ref, False),
            (rb_ref, vb_ref, kb_ref, lwb_ref, bb_ref, kdb_ref, yb_ref, True))
    probs, waves = [], [[] for _ in range(RW_CHUNKS_PER_STEP)]
    for d, (r_ref, v_ref, k_ref, lw_ref, b_ref, kd_ref, y_ref, rev) in enumerate(dirs):
        for w in range(RW_CHUNKS_PER_STEP):
            ci = RW_CHUNKS_PER_STEP - 1 - w if rev else w
            rows = slice(ci * c, (ci + 1) * c)
            for g in range(n_grp):
                sl = slice(g * gl, (g + 1) * gl)
                waves[w].append((len(probs), y_ref, rows, sl, d * n_grp + g))
                probs.append((r_ref[rows, sl], v_ref[rows, sl], k_ref[rows, sl], lw_ref[0][rows, sl],
                              b_ref[0][rows, sl], kd_ref[0][rows, sl], rev))
    with_state = _rw_group_steps(probs)
    for wave in waves:
        res = with_state([p for p, *_ in wave], [h_ref[idx].astype(BF16) for *_, idx in wave])
        for (_, y_ref, rows, sl, idx), (y, dh, dec) in zip(wave, res):
            y_ref[rows, sl] = y
            h_ref[idx] = (h_ref[idx] + jnp.where(blk, dh, 0.0)) * dec


def _rw_chunks(r, v, kkn, lw, beta, kd, n_ctx):
    t = r.shape[0]
    c = RW_CHUNK * RW_CHUNKS_PER_STEP
    nc = t // c
    nc_ctx = n_ctx // c

    def fwd(i):
        return i

    def bwd(i):
        return jnp.where(i < nc_ctx, nc_ctx - 1 - i, nc - 1 - (i - nc_ctx))

    def specs(m):
        one = pl.BlockSpec((c, RW_WIDTH), lambda i: (m(i), 0))
        return one

    def specs2(m, d):
        return pl.BlockSpec((1, c, RW_WIDTH), lambda i: (d, m(i), 0))

    in_specs = [specs(fwd), specs(fwd), specs(fwd), specs2(fwd, 0), specs2(fwd, 0), specs2(fwd, 0),
                specs(bwd), specs(bwd), specs(bwd), specs2(bwd, 1), specs2(bwd, 1), specs2(bwd, 1)]
    f = jax.ShapeDtypeStruct((t, RW_WIDTH), F32)
    n_state = 2 * RW_HEADS // RW_GROUP
    gl = RW_GROUP * RW_HEAD_DIM
    return pl.pallas_call(
        _rw_chunk_body,
        grid=(nc,),
        in_specs=in_specs,
        out_specs=[specs(fwd), specs(bwd)],
        out_shape=[f, f],
        scratch_shapes=[pltpu.VMEM((n_state, gl, gl), F32)],
        compiler_params=_cp(1),
        name="rw_chunks",
    )(r, v, kkn, lw, beta, kd, r, v, kkn, lw, beta, kd)


def _merge_body(n_ctx, t_off, x_ref, mod_ref, g1_ref, g2n_ref, wg_ref, yat_ref, yb_ref, yf_ref, ybw_ref, r_ref, v_ref,
                kd_ref, gd_ref, g2_ref, rk_ref, lng_ref, lnb_ref, seg_ref, woa_ref, wob_ref, wor_ref, wout_ref,
                xmid_ref, h2_ref):
    tm, d = x_ref.shape
    row0 = (pl.program_id(0) + t_off) * tm
    x = x_ref[...]
    mod = mod_ref[...]
    h = _modulated_norm(x, g1_ref[...], mod, row0, n_ctx, 1).astype(BF16)
    gates = jax.nn.sigmoid(_dot(h, wg_ref[...]))
    pa = _dot_tn(yat_ref[...].astype(BF16), woa_ref[...])
    pb = _dot(yb_ref[...].astype(BF16), wob_ref[...])
    seg = seg_ref[...]
    inv = 1.0 / RW_HEAD_DIM
    y = yf_ref[...] + ybw_ref[...]
    dy = y - _seg_sum(y, seg) * inv
    yn = dy * lax.rsqrt(_seg_sum(dy * dy, seg) * inv + RW_GN_EPS) * lng_ref[...] + lnb_ref[...]
    bonus = _seg_sum(r_ref[...] * (kd_ref[0] + kd_ref[1]) * rk_ref[...], seg) * v_ref[...]
    g = _dot(jax.nn.sigmoid(gd_ref[...]).astype(BF16), g2_ref[...])
    pr = _dot(((yn + bonus) * g).astype(BF16), wor_ref[...])
    m = gates[:, :d] * pa + gates[:, d:2 * d] * pb + gates[:, 2 * d:] * pr
    o = _dot(m.astype(BF16), wout_ref[...])
    x_mid = x + _gate_rows(mod, row0, tm, n_ctx, 1) * o
    xmid_ref[...] = x_mid
    h2_ref[...] = _modulated_norm(x_mid, g2n_ref[...], mod, row0, n_ctx, 2).astype(h2_ref.dtype)


def _merge(x_all, mod_l, g1, g2n, wg, yat, yb, yf, ybw, r, v, kd, gd, g2, rk, lng, lnb, seg, woa, wob, wor, wout,
           n_ctx, with_ctx, h2_dtype):
    t, d = x_all.shape
    t_off = 0 if with_ctx else n_ctx // TM
    n_rows = yat.shape[1]
    tok = lambda w: pl.BlockSpec((TM, w), lambda i: (i + t_off, 0))
    own = lambda w: pl.BlockSpec((TM, w), lambda i: (i, 0))
    smalls = (g2, rk, lng, lnb, seg, woa, wob, wor, wout)
    return pl.pallas_call(
        functools.partial(_merge_body, n_ctx, t_off),
        grid=(n_rows // TM,),
        in_specs=[tok(d), _full(mod_l.shape), _full(g1.shape), _full(g2n.shape), _full(wg.shape),
                  pl.BlockSpec((yat.shape[0], TM), lambda i: (0, i)), own(NA_WIDTH),
                  tok(RW_WIDTH), tok(RW_WIDTH), tok(RW_WIDTH), tok(RW_WIDTH),
                  pl.BlockSpec((2, TM, RW_WIDTH), lambda i: (0, i + t_off, 0)), tok(RW_GATE_LORA)]
                 + [_full(a.shape) for a in smalls],
        out_specs=[own(d), own(d)],
        out_shape=[jax.ShapeDtypeStruct((n_rows, d), F32), jax.ShapeDtypeStruct((n_rows, d), h2_dtype)],
        compiler_params=_cp(1),
        name="merge",
    )(x_all, mod_l, g1, g2n, wg, yat, yb, yf, ybw, r, v, kd, gd, *smalls)


def _ffn_body(n_ctx, h_ref, x_ref, mod_ref, w1_ref, w3_ref, w2_ref, o_ref):
    tm = h_ref.shape[0]
    h = h_ref[...]
    acc = jnp.zeros(o_ref.shape, F32)
    for f in range(w1_ref.shape[0]):
        a = _dot(h, w1_ref[f])
        b = _dot(h, w3_ref[f])
        acc = acc + _dot((a * jax.nn.sigmoid(a) * b).astype(BF16), w2_ref[f])
    o_ref[...] = x_ref[...] + _gate_rows(mod_ref[...], pl.program_id(0) * tm, tm, n_ctx, 2) * acc


def _ffn(h2, x_mid, mod_l, w1, w3, w2, n_ctx):
    t, d = x_mid.shape
    tok = pl.BlockSpec((TM, d), lambda i: (i, 0))
    return pl.pallas_call(
        functools.partial(_ffn_body, n_ctx),
        grid=(t // TM,),
        in_specs=[tok, tok, _full(mod_l.shape), _full(w1.shape), _full(w3.shape), _full(w2.shape)],
        out_specs=tok,
        out_shape=jax.ShapeDtypeStruct((t, d), F32),
        compiler_params=_cp(1),
        name="ffn",
    )(h2, x_mid, mod_l, w1, w3, w2)


def _router_body(h_ref, w_ref, idx_ref, gate_ref):
    logits = _dot(h_ref[...].astype(BF16), w_ref[...])
    lane_i = lax.broadcasted_iota(jnp.int32, logits.shape, 1)
    lane = lane_i.astype(F32)
    logits = jnp.where(lane_i < N_EXPERTS, logits, NEG_BIG)
    m1 = jnp.max(logits, axis=-1, keepdims=True)
    i1 = jnp.min(jnp.where(logits == m1, lane, float(LANE)), axis=-1, keepdims=True)
    rest = jnp.where(lane == i1, NEG_BIG, logits)
    m2 = jnp.max(rest, axis=-1, keepdims=True)
    i2 = jnp.min(jnp.where(rest == m2, lane, float(LANE)), axis=-1, keepdims=True)
    e2 = jnp.exp(m2 - m1)
    g1 = 1.0 / (1.0 + e2)
    lane = lane_i
    idx_ref[...] = jnp.where(lane == 0, i1, jnp.where(lane == 1, i2, 0.0)).astype(jnp.int32)
    gate_ref[...] = jnp.where(lane == 0, g1, jnp.where(lane == 1, e2 * g1, 0.0))


def _router(h2, w_router_pad):
    n, d = h2.shape
    tok = lambda w: pl.BlockSpec((TM, w), lambda i: (i, 0))
    return pl.pallas_call(
        _router_body,
        grid=(n // TM,),
        in_specs=[tok(d), _full(w_router_pad.shape)],
        out_specs=[tok(LANE), tok(LANE)],
        out_shape=[jax.ShapeDtypeStruct((n, LANE), jnp.int32), jax.ShapeDtypeStruct((n, LANE), F32)],
        compiler_params=_cp(1),
        name="router",
    )(h2, w_router_pad)


def _row_copy(src_hbm, dst_vmem, src_row, dst_row, sem):
    return pltpu.make_async_copy(src_hbm.at[pl.ds(src_row, 1)], dst_vmem.at[pl.ds(dst_row, 1)], sem)


def _moe_ffn_body(be_ref, nb_ref, tok_hbm, h_hbm, w1_ref, w3_ref, w2_ref, o_ref, x_buf, acc_ref, idx_smem,
                  sem_idx, sem_x):
    b, f = pl.program_id(0), pl.program_id(1)
    n_used = nb_ref[0]
    bm = x_buf.shape[1]
    cur, nxt = b % 2, (b + 1) % 2

    def idx_copy(blk, slot):
        return pltpu.make_async_copy(tok_hbm.at[blk], idx_smem.at[slot], sem_idx)

    def start_rows(slot):
        for r in range(bm):
            _row_copy(h_hbm, x_buf.at[slot], idx_smem[slot, r], r, sem_x.at[slot]).start(priority=r % 2)

    def wait_rows(slot):
        pltpu.make_async_copy(h_hbm.at[pl.ds(0, bm)], x_buf.at[slot], sem_x.at[slot]).wait()

    def half(first):
        x = x_buf[cur].astype(BF16)
        a = _dot(x, w1_ref[0])
        g = _dot(x, w3_ref[0])
        y = _dot((a * jax.nn.sigmoid(a) * g).astype(BF16), w2_ref[0])
        if first:
            acc_ref[...] = y
        else:
            o_ref[...] = acc_ref[...] + y

    @pl.when((f == 0) & (b == 0))
    def _():
        cp = idx_copy(0, 0)
        cp.start()
        cp.wait()
        start_rows(0)

    @pl.when((f == 0) & (b < n_used))
    def _():
        wait_rows(cur)

        @pl.when(b + 1 < n_used)
        def _():
            idx_copy(b + 1, nxt).start()

        half(True)

    @pl.when((f == 1) & (b + 1 < n_used))
    def _():
        idx_copy(b + 1, nxt).wait()
        start_rows(nxt)
        half(False)

    @pl.when((f == 1) & (b + 1 == n_used))
    def _():
        half(False)

    @pl.when((f == 1) & (b >= n_used))
    def _():
        o_ref[...] = jnp.zeros_like(o_ref)


def _moe_ffn(blk_expert, n_used, tok_of_slot, h2, w1, w3, w2, fc):
    nb = tok_of_slot.shape[0]
    d = h2.shape[1]
    assert w1.shape[2] == 2 * fc
    grid_spec = pltpu.PrefetchScalarGridSpec(
        num_scalar_prefetch=2,
        grid=(nb, 2),
        in_specs=[pl.BlockSpec(memory_space=pl.ANY), pl.BlockSpec(memory_space=pl.ANY),
                  pl.BlockSpec((1, d, fc), lambda b, f, be, nu: (be[b], 0, f)),
                  pl.BlockSpec((1, d, fc), lambda b, f, be, nu: (be[b], 0, f)),
                  pl.BlockSpec((1, fc, d), lambda b, f, be, nu: (be[b], f, 0))],
        out_specs=pl.BlockSpec((MOE_BM, d), lambda b, f, be, nu: (b, 0)),
        scratch_shapes=[pltpu.VMEM((2, MOE_BM, d), F32), pltpu.VMEM((MOE_BM, d), F32),
                        pltpu.SMEM((2, MOE_BM), jnp.int32), pltpu.SemaphoreType.DMA(()),
                        pltpu.SemaphoreType.DMA((2,))],
    )
    return pl.pallas_call(
        _moe_ffn_body,
        grid_spec=grid_spec,
        out_shape=jax.ShapeDtypeStruct((nb * MOE_BM, d), F32),
        compiler_params=_cp(2),
        name="moe_ffn",
    )(blk_expert, n_used, tok_of_slot, h2, w1, w3, w2)


def _combine_body(slot_hbm, yg_hbm, gate_ref, x_ref, mod_ref, o_ref, y_buf, idx_smem, sem_idx, sem_rows):
    i = pl.program_id(0)
    nt = pl.num_programs(0)
    tm, d = x_ref.shape
    copies = [pltpu.make_async_copy(slot_hbm.at[c * nt + i], idx_smem.at[c], sem_idx.at[c]) for c in range(TOP_K)]
    for cp in copies:
        cp.start()
    for cp in copies:
        cp.wait()

    def start(g, carry):
        for u in range(GATHER_UNROLL):
            r = g * GATHER_UNROLL + u
            for c in range(TOP_K):
                _row_copy(yg_hbm, y_buf.at[c], idx_smem[c, r], r, sem_rows).start(priority=c)
        return carry

    lax.fori_loop(0, tm // GATHER_UNROLL, start, 0)
    for c in range(TOP_K):
        pltpu.make_async_copy(yg_hbm.at[pl.ds(0, tm)], y_buf.at[c], sem_rows).wait()
    g = gate_ref[...]
    f = g[:, 0:1] * y_buf[0] + g[:, 1:2] * y_buf[1]
    o_ref[...] = x_ref[...] + mod_ref[0:1, 5 * d:6 * d] * f


def _moe_combine(slot, yg, gate_pad, x_mid, mod_l):
    n, d = x_mid.shape
    nt = n // TM
    return pl.pallas_call(
        _combine_body,
        grid=(nt,),
        in_specs=[pl.BlockSpec(memory_space=pl.ANY), pl.BlockSpec(memory_space=pl.ANY),
                  pl.BlockSpec((TM, LANE), lambda i: (i, 0)), pl.BlockSpec((TM, d), lambda i: (i, 0)),
                  _full(mod_l.shape)],
        out_specs=pl.BlockSpec((TM, d), lambda i: (i, 0)),
        out_shape=jax.ShapeDtypeStruct((n, d), F32),
        scratch_shapes=[pltpu.VMEM((TOP_K, TM, d), F32), pltpu.SMEM((TOP_K, TM), jnp.int32),
                        pltpu.SemaphoreType.DMA((TOP_K,)), pltpu.SemaphoreType.DMA(())],
        compiler_params=_cp(1),
        name="moe_combine",
    )(slot.reshape(TOP_K * nt, TM), yg, gate_pad, x_mid, mod_l)


def _moe(h2, x_mid, mod_l, router_w, w1, w3, w2, fc):
    n, d = h2.shape
    e_n = router_w.shape[1]
    idx_pad, gate_pad = _router(h2, jnp.zeros((d, LANE), BF16).at[:, :e_n].set(router_w.astype(BF16)))
    flat_e = idx_pad[:, :TOP_K].T.reshape(-1)
    onehot = (flat_e[:, None] == jnp.arange(e_n)[None, :]).astype(jnp.int32)
    rank = jnp.take_along_axis(jnp.cumsum(onehot, axis=0) - onehot, flat_e[:, None], axis=1)[:, 0]
    counts = jnp.sum(onehot, axis=0)
    padded = (counts + MOE_BM - 1) // MOE_BM * MOE_BM
    pad_ends = jnp.cumsum(padded)
    slot = ((pad_ends - padded)[flat_e] + rank).astype(jnp.int32)
    nb = TOP_K * n // MOE_BM + e_n
    n_slots = nb * MOE_BM
    tok_of_slot = jnp.zeros((n_slots,), jnp.int32).at[slot].set(jnp.tile(jnp.arange(n, dtype=jnp.int32), TOP_K))
    blk_expert = jnp.minimum(jnp.searchsorted(pad_ends, jnp.arange(nb, dtype=jnp.int32) * MOE_BM, side="right"),
                             e_n - 1).astype(jnp.int32)
    n_used = (pad_ends[-1:] // MOE_BM).astype(jnp.int32)
    yg = _moe_ffn(blk_expert, n_used, tok_of_slot.reshape(nb, MOE_BM), h2, w1, w3, w2, fc)
    return _moe_combine(slot, yg, gate_pad, x_mid, mod_l)


def _head_lanes(nope=None, rope=None):
    half = MLA_ROPE // 4
    lead = (nope if nope is not None else rope).shape[:-1]
    dt = (nope if nope is not None else rope).dtype
    z = lambda n: jnp.zeros(lead + (n,), dt)
    n_lo = 64 - 2 * half
    nope_lo, nope_hi = (nope[..., :n_lo], nope[..., n_lo:]) if nope is not None else (z(n_lo), z(MLA_NOPE - n_lo))
    r1, r2, c1, c2 = ([rope[..., i * half:(i + 1) * half] for i in range(4)] if rope is not None else [z(half)] * 4)
    return jnp.concatenate([r1, c1, nope_lo, r2, c2, nope_hi, z(HEAD_PAD - MLA_QK)], axis=-1)


def _rope_rows():
    half = MLA_ROPE // 4
    freqs = np.exp(-math.log(ROPE_BASE) * np.arange(half, dtype=np.float32) / half).astype(np.float32)
    tab = np.zeros((SUBLANE, HEAD_PAD), np.float32)
    for base, sign in ((0, -1.0), (64, 1.0)):
        tab[0, base:base + half] = freqs
        tab[1, base + half:base + 2 * half] = freqs
        tab[2, base:base + 2 * half] = sign
    return jnp.asarray(tab)


def _seg_matrix(width, seg):
    i = np.arange(width)
    return jnp.asarray((i[:, None] // seg == i[None, :] // seg).astype(np.float32), dtype=BF16)


def _block_diag2(w):
    _, r, c = w.shape
    z = jnp.zeros((r, c), w.dtype)
    return jnp.concatenate([jnp.concatenate([w[0], z], axis=1), jnp.concatenate([z, w[1]], axis=1)], axis=0)


def kernel(x, c, ctx, c_ctx, mod_w, mod_b, norm1_g, norm2_g, w_in, mla_cq_g, mla_wuq, mla_ckv_g, mla_wukv, mla_qn_g, mla_kn_g, mla_wo, na_qn_g, na_kn_g, na_rpb, na_wo, rw_mu, rw_w0, rw_w2, rw_a0, rw_a2, rw_g2, rw_kk, rw_ka, rw_rk, rw_ln_g, rw_ln_b, rw_wo, w_out, ffn_w1, ffn_w3, ffn_w2, moe_router, moe_w1, moe_w3, moe_w2):
    b, n_lat, d = x.shape
    n_ctx = ctx.shape[1]
    depth = mod_w.shape[0]
    assert b == 1 and n_ctx % TM == 0 and n_lat % TM == 0 and n_lat % (NA_ROWS * GRID_W) == 0
    assert n_lat // GRID_W >= NA_KROWS and n_ctx % (RW_CHUNK * RW_CHUNKS_PER_STEP) == 0

    mod = _modulation(c, c_ctx, mod_w, mod_b)
    x_all = jnp.concatenate([ctx[0], x[0]], axis=0)
    rope_rows = _rope_rows()
    seg64 = _seg_matrix(RW_WIDTH, RW_HEAD_DIM)

    o_ckv = MLA_Q_LORA
    o_kr = o_ckv + MLA_KV_LORA
    o_na = o_kr + MLA_ROPE
    o_rz = o_na + 3 * NA_WIDTH
    o_gate = o_rz + RW_IN

    for l in range(depth):
        need_ctx = l < depth - 1
        mod_l = mod[l]
        g1 = norm1_g[l][None, :]
        g2n = norm2_g[l][None, :]
        wi = w_in[l]

        w_a = jnp.concatenate([wi[:, :o_kr], _head_lanes(rope=wi[:, o_kr:o_na])], axis=1).astype(BF16)
        w_na = wi[:, o_na:o_rz].astype(BF16)
        w_rz = wi[:, o_rz:o_gate].astype(BF16)
        w_g = wi[:, o_gate:].astype(BF16)
        wuq3 = mla_wuq[l].reshape(MLA_Q_LORA, MLA_HEADS, MLA_QK)
        wuq = _head_lanes(wuq3[..., :MLA_NOPE], wuq3[..., MLA_NOPE:]).reshape(MLA_Q_LORA, -1).astype(BF16)
        wukv = mla_wukv[l].reshape(MLA_KV_LORA, MLA_HEADS, MLA_NOPE + MLA_V)
        wuk = _head_lanes(nope=wukv[:, :, :MLA_NOPE]).reshape(MLA_KV_LORA, -1).astype(BF16)
        wuvt = wukv[:, :, MLA_NOPE:].reshape(MLA_KV_LORA, -1).T.astype(BF16)
        qng = _head_lanes(mla_qn_g[l][:MLA_NOPE], mla_qn_g[l][MLA_NOPE:])[None, :]
        kng = _head_lanes(mla_kn_g[l][:MLA_NOPE], mla_kn_g[l][MLA_NOPE:])[None, :]
        bound = 1.01 * math.sqrt(MLA_QK) * jnp.max(jnp.abs(mla_qn_g[l])) * jnp.max(jnp.abs(mla_kn_g[l]))
        shift = jnp.zeros((1, HEAD_PAD), F32).at[0, MLA_QK].set(-bound * math.log2(math.e))
        mla_wts = (w_a, mla_cq_g[l][None, :], wuq, mla_ckv_g[l][None, :], wuk, wuvt, qng, kng, shift, rope_rows)

        qa, ka, vta = _mla_proj(x_all, mod_l, g1, mla_wts, n_ctx)
        nt_ctx, nt_all = n_ctx // TM, (n_ctx + n_lat) // TM

        def mla_attention(fast):
            def run(qa, ka, vta):
                y = _mla_attn(qa, ka, vta, nt_ctx, nt_all - nt_ctx, nt_all, fast)
                if need_ctx:
                    y = jnp.concatenate([_mla_attn(qa, ka, vta, 0, nt_ctx, nt_ctx, fast), y], axis=1)
                return y
            return run

        yat = lax.cond(bound <= MLA_FAST_BOUND, mla_attention(True), mla_attention(False), qa, ka, vta)

        qb, kb, vb = _na_proj(x_all, mod_l, g1, w_na, jnp.tile(na_qn_g[l], NA_HEADS)[None, :],
                              jnp.tile(na_kn_g[l], NA_HEADS)[None, :], seg64, n_ctx)
        yb = _na_attn(qb, kb, vb, _na_bias_tables(na_rpb[l]), n_ctx)
        if need_ctx:
            yb = jnp.concatenate([_ctx_attn(qb, kb, vb, n_ctx), yb], axis=0)

        r, v, kkn, lw, beta, kd, gd = _rw_prep(
            x_all, mod_l, g1, w_rz, rw_mu[l], rw_w0[l].reshape(1, -1), _block_diag2(rw_w2[l]).astype(BF16), rw_a0[l].reshape(1, -1),
            _block_diag2(rw_a2[l]).astype(BF16), rw_kk[l][None, :], rw_ka[l][None, :], seg64, n_ctx)
        yf, ybw = _rw_chunks(r, v, kkn, lw, beta, kd, n_ctx)

        x_mid, h2 = _merge(x_all, mod_l, g1, g2n, w_g, yat, yb, yf, ybw, r, v, kd, gd,
                           rw_g2[l].astype(BF16), rw_rk[l].reshape(1, -1), rw_ln_g[l][None, :], rw_ln_b[l][None, :], seg64,
                           mla_wo[l].astype(BF16), na_wo[l].astype(BF16), rw_wo[l].astype(BF16), w_out[l].astype(BF16),
                           n_ctx, need_ctx, BF16 if l % 2 == 0 else F32)

        if l % 2 == 0:
            fc = FFN_FC
            w1 = ffn_w1[l // 2]
            nf = w1.shape[1] // fc
            w1r = w1.reshape(d, nf, fc).transpose(1, 0, 2).astype(BF16)
            w3r = ffn_w3[l // 2].reshape(d, nf, fc).transpose(1, 0, 2).astype(BF16)
            w2r = ffn_w2[l // 2].reshape(nf, fc, d).astype(BF16)
            if need_ctx:
                x_all = _ffn(h2, x_mid, mod_l, w1r, w3r, w2r, n_ctx)
            else:
                x_lat = _ffn(h2, x_mid, mod_l, w1r, w3r, w2r, 0)
        else:
            moe_args = (moe_router[l // 2], moe_w1[l // 2].astype(BF16), moe_w3[l // 2].astype(BF16),
                        moe_w2[l // 2].astype(BF16), MOE_FC)
            if need_ctx:
                lat = _moe(h2[n_ctx:], x_mid[n_ctx:], mod_l, *moe_args)
                ctx_rows = _moe(h2[:n_ctx], x_mid[:n_ctx], mod_l.at[0].set(mod_l[1]), *moe_args)
                x_all = jnp.concatenate([ctx_rows, lat], axis=0)
            else:
                x_lat = _moe(h2, x_mid, mod_l, *moe_args)
    return x_lat[None]
```

```python
import functools
import math

import numpy as np
import jax
import jax.numpy as jnp
from jax import lax
from jax.experimental import pallas as pl
from jax.experimental.pallas import tpu as pltpu

F32 = jnp.float32
BF16 = jnp.bfloat16

GRID_W = 64
MLA_HEADS, MLA_NOPE, MLA_ROPE, MLA_V = 8, 64, 32, 64
MLA_QK = MLA_NOPE + MLA_ROPE
MLA_Q_LORA, MLA_KV_LORA = 384, 256
NA_HEADS, NA_HEAD_DIM = 8, 64
NA_WIDTH = NA_HEADS * NA_HEAD_DIM
NA_WIN_ROWS, NA_WIN_COLS = 8, 16
RW_HEADS, RW_HEAD_DIM = 8, 64
RW_WIDTH = RW_HEADS * RW_HEAD_DIM
RW_DECAY_LORA, RW_ICLR_LORA, RW_GATE_LORA = 64, 64, 128
RW_GN_EPS = 64e-5
RW_IN = 3 * RW_WIDTH + 2 * RW_DECAY_LORA + 2 * RW_ICLR_LORA + RW_GATE_LORA
N_EXPERTS, TOP_K = 8, 2
ROPE_BASE = 10000.0
NORM_EPS = 1e-6

LANE = 128
SUBLANE = 8
VMEM_LIMIT = 56 * 1024 * 1024

TM = 256
PROJ_ROW_GROUPS = 2
HEAD_PAD = LANE
MLA_Q_STREAMS = 2
MLA_FAST_UNROLL = 65
MLA_KEY_TILES_PER_STEP = 5
MLA_FAST_BOUND = 40.0
RW_CHUNK = 64
RW_GROUP = 4
RW_CHUNKS_PER_STEP = 4
NA_ROWS = 4
NA_KROWS = NA_ROWS + NA_WIN_ROWS
MOE_BM = 512
MOE_FC = 1792
GATHER_UNROLL = 64
FFN_FC = 1408
NEG_BIG = -1e30


def _cp(n_axes, vmem=VMEM_LIMIT):
    return pltpu.CompilerParams(dimension_semantics=("arbitrary",) * n_axes, vmem_limit_bytes=vmem)


def _dot(a, b):
    return jnp.dot(a, b, preferred_element_type=F32)


def _dot_nt(a, b):
    return lax.dot_general(a, b, (((1,), (1,)), ((), ())), preferred_element_type=F32)


def _dot_tn(a, b):
    return lax.dot_general(a, b, (((0,), (0,)), ((), ())), preferred_element_type=F32)


def _split2(x):
    hi = x.astype(BF16)
    lo = (x - hi.astype(F32)).astype(BF16)
    return hi, lo


def _split3(x):
    hi = x.astype(BF16)
    r1 = x - hi.astype(F32)
    mid = r1.astype(BF16)
    lo = (r1 - mid.astype(F32)).astype(BF16)
    return hi, mid, lo


def _seg_sum(x, seg01):
    hi, lo = _split2(x)
    return _dot(hi, seg01) + _dot(lo, seg01)


def _full(shape):
    nd = len(shape)
    return pl.BlockSpec(shape, lambda *_: (0,) * nd)


def _modulated_norm(x, g, mod, first_row, n_ctx, which):
    d = x.shape[1]
    off = 0 if which == 1 else 3 * d
    y = x * lax.rsqrt(jnp.mean(x * x, axis=-1, keepdims=True) + NORM_EPS) * g
    rows = first_row + lax.broadcasted_iota(jnp.int32, (x.shape[0], 1), 0)
    is_ctx = rows < n_ctx
    sh = jnp.where(is_ctx, mod[1:2, off:off + d], mod[0:1, off:off + d])
    sc = jnp.where(is_ctx, mod[1:2, off + d:off + 2 * d], mod[0:1, off + d:off + 2 * d])
    return y * (1.0 + sc) + sh


def _gate_rows(mod, first_row, n_rows, n_ctx, which):
    d = mod.shape[1] // 6
    off = 2 * d if which == 1 else 5 * d
    rows = first_row + lax.broadcasted_iota(jnp.int32, (n_rows, 1), 0)
    return jnp.where(rows < n_ctx, mod[1:2, off:off + d], mod[0:1, off:off + d])


def _mod_body(cc_ref, w_ref, b_ref, o_ref):
    cc = cc_ref[...]
    s = cc * jax.nn.sigmoid(cc)
    o_ref[0] = _dot(s.astype(BF16), w_ref[0].astype(BF16)) + b_ref[0]


def _modulation(c, c_ctx, mod_w, mod_b):
    depth, d, d6 = mod_w.shape
    nt = d6 // 4
    cc = jnp.zeros((SUBLANE, d), F32).at[0].set(c[0]).at[1].set(c_ctx)
    return pl.pallas_call(
        _mod_body,
        grid=(depth, d6 // nt),
        in_specs=[_full((SUBLANE, d)),
                  pl.BlockSpec((1, d, nt), lambda l, j: (l, 0, j)),
                  pl.BlockSpec((1, 1, nt), lambda l, j: (l, 0, j))],
        out_specs=pl.BlockSpec((1, SUBLANE, nt), lambda l, j: (l, 0, j)),
        out_shape=jax.ShapeDtypeStruct((depth, SUBLANE, d6), F32),
        compiler_params=_cp(2),
        name="modulation",
    )(cc, mod_w, mod_b.reshape(depth, 1, d6))


def _rms(x, width):
    return x * lax.rsqrt(jnp.sum(x * x, axis=-1, keepdims=True) * (1.0 / width) + NORM_EPS)


def _mla_proj_rows(n_ctx, r0, nr, x_ref, mod_ref, g1_ref, wa_ref, cqg_ref, wuq_ref, ckvg_ref, wuk_ref, wuvt_ref,
                   qng_ref, kng_ref, shift_ref, rope_ref, q_ref, k_ref, vt_ref):
    rs = slice(r0, r0 + nr)
    row0 = pl.program_id(0) * x_ref.shape[0] + r0
    h = _modulated_norm(x_ref[rs, :], g1_ref[...], mod_ref[...], row0, n_ctx, 1).astype(BF16)
    yield
    z = _dot(h, wa_ref[...])
    cq = z[:, :MLA_Q_LORA]
    ckv = z[:, MLA_Q_LORA:MLA_Q_LORA + MLA_KV_LORA]
    kr = z[:, MLA_Q_LORA + MLA_KV_LORA:]
    cqn = (_rms(cq, MLA_Q_LORA) * cqg_ref[...]).astype(BF16)
    ckvn = (_rms(ckv, MLA_KV_LORA) * ckvg_ref[...]).astype(BF16)
    yield
    q = _dot(cqn, wuq_ref[...])
    kn = _dot(ckvn, wuk_ref[...])
    vt_ref[0, :, rs] = _dot_nt(wuvt_ref[...], ckvn).astype(BF16)
    rows = row0 + lax.broadcasted_iota(jnp.int32, (nr, 1), 0)
    pos = jnp.maximum(rows - n_ctx, 0)
    ang = ((pos // GRID_W).astype(F32) * rope_ref[0:1] + (pos % GRID_W).astype(F32) * rope_ref[1:2])
    ang = jnp.where(rows >= n_ctx, ang, 0.0)
    cos = jnp.cos(ang)
    sin_s = jnp.sin(ang) * rope_ref[2:3]
    yield

    def rope(x):
        return x * cos + pltpu.roll(x, HEAD_PAD // 2, 1) * sin_s

    q_scale = MLA_QK ** -0.5 * math.log2(math.e)
    one_lane = jnp.where(lax.broadcasted_iota(jnp.int32, (1, HEAD_PAD), 1) == MLA_QK, 1.0, 0.0)
    sls = [slice(hh * HEAD_PAD, (hh + 1) * HEAD_PAD) for hh in range(MLA_HEADS)]
    qn = [_rms(q[:, sl], MLA_QK) * qng_ref[...] for sl in sls]
    kk = [_rms(kn[:, sl] + kr, MLA_QK) * kng_ref[...] for sl in sls]
    yield
    qr = [rope(x) * q_scale + shift_ref[...] for x in qn]
    kr_ = [rope(x) + one_lane for x in kk]
    yield
    for hh, sl in enumerate(sls):
        q_ref[rs, sl] = qr[hh].astype(BF16)
        k_ref[rs, sl] = kr_[hh].astype(BF16)


def _round_robin(gens):
    while gens:
        gens = [g for g in gens if next(g, StopIteration) is not StopIteration]


def _mla_proj_body(n_ctx, *refs):
    tm = refs[0].shape[0]
    nr = tm // PROJ_ROW_GROUPS
    _round_robin([_mla_proj_rows(n_ctx, g * nr, nr, *refs) for g in range(PROJ_ROW_GROUPS)])


def _mla_proj(x_all, mod_l, g1, wts, n_ctx):
    t, d = x_all.shape
    nt = t // TM
    hp = MLA_HEADS * HEAD_PAD
    tok = lambda w: pl.BlockSpec((TM, w), lambda i: (i, 0))
    return pl.pallas_call(
        functools.partial(_mla_proj_body, n_ctx),
        grid=(nt,),
        in_specs=[tok(d), _full(mod_l.shape), _full(g1.shape)] + [_full(w.shape) for w in wts],
        out_specs=[tok(hp), tok(hp), pl.BlockSpec((1, MLA_HEADS * MLA_V, TM), lambda i: (i, 0, 0))],
        out_shape=[jax.ShapeDtypeStruct((t, hp), BF16), jax.ShapeDtypeStruct((t, hp), BF16),
                   jax.ShapeDtypeStruct((nt, MLA_HEADS * MLA_V, TM), BF16)],
        compiler_params=_cp(1),
        name="mla_proj",
    )(x_all, mod_l, g1, *wts)


def _mla_attn_fast_body(unroll, n_streams, *refs):
    q_refs, (k_ref, vt_ref, o_ref) = refs[:n_streams], refs[n_streams:]
    tq = q_refs[0].shape[0]
    nk, _, tk = vt_ref.shape
    qts = [q_ref[...].astype(F32).T.astype(BF16) for q_ref in q_refs]

    def scores(j):
        j = jnp.minimum(j, nk - 1)
        kj = k_ref[pl.ds(pl.multiple_of(j * tk, tk), tk), :]
        return tuple(_dot(kj, qt) for qt in qts)

    def probs(ss, ls):
        ps = tuple(jnp.exp2(s) for s in ss)
        ls = tuple(l + jnp.sum(p.reshape(tk // SUBLANE, SUBLANE, tq), axis=0) for l, p in zip(ls, ps))
        return tuple(p.astype(BF16) for p in ps), ls

    def step(it, carry):
        accs, ls, ss, ps = carry
        for u in range(unroll):
            j = it * unroll + u
            s_new = scores(j + 2)
            p_new, ls_new = probs(ss, ls)
            ls = tuple(jnp.where(j + 1 < nk, ln, l) for ln, l in zip(ls_new, ls))
            vj = vt_ref[j]
            accs = tuple(a + _dot(vj, p) for a, p in zip(accs, ps))
            ss, ps = s_new, p_new
        return accs, ls, ss, ps

    zeros_l = tuple(jnp.zeros((SUBLANE, tq), F32) for _ in range(n_streams))
    p0, l0 = probs(scores(0), zeros_l)
    init = (tuple(jnp.zeros((MLA_V, tq), F32) for _ in range(n_streams)), l0, scores(1), p0)
    accs, ls, _, _ = lax.fori_loop(0, nk // unroll, step, init)
    for i in range(n_streams):
        o_ref[:, i * tq:(i + 1) * tq] = accs[i] / jnp.sum(ls[i], axis=0, keepdims=True)


def _mla_attn_online_body(kb, q_ref, k_ref, vt_ref, o_ref):
    tq = q_ref.shape[0]
    nk, _, tk = vt_ref.shape
    nblk = nk // kb
    rows = kb * tk
    q = q_ref[...]

    def scores(j):
        return _dot_nt(k_ref[pl.ds(pl.multiple_of(j * rows, rows), rows), :], q)

    def consume(j, m, l, acc, s):
        m_new = jnp.maximum(m, jnp.max(s, axis=0, keepdims=True))
        alpha = jnp.exp2(m - m_new)
        p = jnp.exp2(s - m_new)
        l = alpha * l + jnp.sum(p, axis=0, keepdims=True)
        pb = p.astype(BF16)
        pv = _dot(vt_ref[j * kb], pb[0:tk])
        for i in range(1, kb):
            pv = pv + _dot(vt_ref[j * kb + i], pb[i * tk:(i + 1) * tk])
        return m_new, l, alpha * acc + pv

    def step(j, carry):
        m, l, acc, s = carry
        s_next = scores(j + 1)
        m, l, acc = consume(j, m, l, acc, s)
        return m, l, acc, s_next

    init = (jnp.full((1, tq), NEG_BIG, F32), jnp.zeros((1, tq), F32), jnp.zeros((MLA_V, tq), F32), scores(0))
    m, l, acc, s = lax.fori_loop(0, nblk - 1, step, init)
    _, l, acc = consume(nblk - 1, m, l, acc, s)
    o_ref[...] = acc / l


def _largest_divisor(n, cap):
    return max(b for b in range(1, cap + 1) if n % b == 0)


def _mla_attn(q, k, vt, q_tile0, nq, nk, fast):
    if fast:
        ns = MLA_Q_STREAMS if nq % MLA_Q_STREAMS == 0 else 1
        body = functools.partial(_mla_attn_fast_body, _largest_divisor(nk, MLA_FAST_UNROLL), ns)
    else:
        ns = 1
        body = functools.partial(_mla_attn_online_body, _largest_divisor(nk, MLA_KEY_TILES_PER_STEP))
    q_specs = [pl.BlockSpec((TM, HEAD_PAD), lambda h, i, s=s: (ns * i + q_tile0 + s, h)) for s in range(ns)]
    return pl.pallas_call(
        body,
        grid=(MLA_HEADS, nq // ns),
        in_specs=q_specs + [pl.BlockSpec((nk * TM, HEAD_PAD), lambda h, i: (0, h)),
                            pl.BlockSpec((nk, MLA_V, TM), lambda h, i: (0, h, 0))],
        out_specs=pl.BlockSpec((MLA_V, ns * TM), lambda h, i: (h, i)),
        out_shape=jax.ShapeDtypeStruct((MLA_HEADS * MLA_V, nq * TM), F32),
        compiler_params=_cp(2),
        name="mla_attn_fast" if fast else "mla_attn_online",
    )(*([q] * ns), k, vt)


def _na_proj_body(n_ctx, x_ref, mod_ref, g1_ref, w_ref, qg_ref, kg_ref, seg_ref, q_ref, k_ref, v_ref):
    tm = x_ref.shape[0]
    h = _modulated_norm(x_ref[...], g1_ref[...], mod_ref[...], pl.program_id(0) * tm, n_ctx, 1).astype(BF16)
    z = _dot(h, w_ref[...])
    seg = seg_ref[...]
    inv = 1.0 / NA_HEAD_DIM

    def head_norm(y, g):
        return y * lax.rsqrt(_seg_sum(y * y, seg) * inv + NORM_EPS) * g

    q_ref[...] = (head_norm(z[:, :NA_WIDTH], qg_ref[...]) * (NA_HEAD_DIM ** -0.5)).astype(BF16)
    k_ref[...] = head_norm(z[:, NA_WIDTH:2 * NA_WIDTH], kg_ref[...]).astype(BF16)
    v_ref[...] = z[:, 2 * NA_WIDTH:].astype(BF16)


def _na_proj(x_all, mod_l, g1, w_na, qg, kg, seg, n_ctx):
    t, d = x_all.shape
    tok = lambda w: pl.BlockSpec((TM, w), lambda i: (i, 0))
    return pl.pallas_call(
        functools.partial(_na_proj_body, n_ctx),
        grid=(t // TM,),
        in_specs=[tok(d), _full(mod_l.shape), _full(g1.shape), _full(w_na.shape), _full(qg.shape), _full(kg.shape),
                  _full(seg.shape)],
        out_specs=[tok(NA_WIDTH)] * 3,
        out_shape=[jax.ShapeDtypeStruct((t, NA_WIDTH), BF16)] * 3,
        compiler_params=_cp(1),
        name="na_proj",
    )(x_all, mod_l, g1, w_na, qg, kg, seg)


def _head_masks():
    lane = lax.broadcasted_iota(jnp.int32, (1, LANE), 1)
    return [(lane // NA_HEAD_DIM == h) for h in range(LANE // NA_HEAD_DIM)]


def _na_attn_body(n_ctx, n_grid_rows, q_ref, k_ref, v_ref, bias_ref, o_ref):
    nq = q_ref.shape[0]
    nkl = NA_KROWS * GRID_W
    g = pl.program_id(1)
    kr0 = jnp.clip(g * NA_ROWS - NA_WIN_ROWS // 2, 0, n_grid_rows - NA_KROWS)
    start = pl.multiple_of(n_ctx + kr0 * GRID_W, GRID_W)
    k_loc = k_ref[pl.ds(start, nkl), :]
    v_loc = v_ref[pl.ds(start, nkl), :]
    k_ctx = k_ref[0:n_ctx, :]
    v_ctx = v_ref[0:n_ctx, :]
    q = q_ref[...]
    out = jnp.zeros((nq, LANE), F32)
    for h, hm in enumerate(_head_masks()):
        qh = jnp.where(hm, q, jnp.zeros_like(q))
        s_loc = _dot_nt(qh, k_loc) + bias_ref[0, h]
        s_ctx = _dot_nt(qh, k_ctx)
        m = jnp.maximum(jnp.max(s_loc, axis=-1, keepdims=True), jnp.max(s_ctx, axis=-1, keepdims=True))
        p_loc = jnp.exp(s_loc - m)
        p_ctx = jnp.exp(s_ctx - m)
        l = jnp.sum(p_loc, axis=-1, keepdims=True) + jnp.sum(p_ctx, axis=-1, keepdims=True)
        o = (_dot(p_loc.astype(BF16), v_loc) + _dot(p_ctx.astype(BF16), v_ctx)) / l
        out = jnp.where(hm, o, out)
    o_ref[...] = out


def _na_attn(q, k, v, bias, n_ctx):
    t = q.shape[0]
    n_lat = t - n_ctx
    n_grid_rows = n_lat // GRID_W
    nq = NA_ROWS * GRID_W
    ng = n_grid_rows // NA_ROWS
    q_blk0 = n_ctx // nq
    npairs = NA_WIDTH // LANE

    def case(g):
        return jnp.where(g == 0, 0, jnp.where(g == ng - 1, 2, 1))

    return pl.pallas_call(
        functools.partial(_na_attn_body, n_ctx, n_grid_rows),
        grid=(npairs, ng),
        in_specs=[pl.BlockSpec((nq, LANE), lambda p, g: (g + q_blk0, p)),
                  pl.BlockSpec((t, LANE), lambda p, g: (0, p)),
                  pl.BlockSpec((t, LANE), lambda p, g: (0, p)),
                  pl.BlockSpec((1, LANE // NA_HEAD_DIM, nq, NA_KROWS * GRID_W), lambda p, g: (case(g), p, 0, 0))],
        out_specs=pl.BlockSpec((nq, LANE), lambda p, g: (g, p)),
        out_shape=jax.ShapeDtypeStruct((n_lat, NA_WIDTH), F32),
        compiler_params=_cp(2),
        name="na_attn",
    )(q, k, v, bias)


def _ctx_attn_body(q_ref, k_ref, v_ref, o_ref):
    q, k, v = q_ref[...], k_ref[...], v_ref[...]
    out = jnp.zeros(o_ref.shape, F32)
    for hm in _head_masks():
        s = _dot_nt(jnp.where(hm, q, jnp.zeros_like(q)), k)
        p = jnp.exp(s - jnp.max(s, axis=-1, keepdims=True))
        o = _dot(p.astype(BF16), v) / jnp.sum(p, axis=-1, keepdims=True)
        out = jnp.where(hm, o, out)
    o_ref[...] = out


def _ctx_attn(q, k, v, n_ctx):
    npairs = NA_WIDTH // LANE
    blk = pl.BlockSpec((n_ctx, LANE), lambda p: (0, p))
    return pl.pallas_call(
        _ctx_attn_body,
        grid=(npairs,),
        in_specs=[blk, blk, blk],
        out_specs=blk,
        out_shape=jax.ShapeDtypeStruct((n_ctx, NA_WIDTH), F32),
        compiler_params=_cp(1),
        name="ctx_attn",
    )(q, k, v)


def _na_bias_tables(rpb):
    h = rpb.shape[0]
    qc = np.arange(GRID_W)[:, None]
    kc = np.arange(GRID_W)[None, :]
    col_start = np.clip(qc - NA_WIN_COLS // 2, 0, GRID_W - NA_WIN_COLS)
    col_ok = (kc >= col_start) & (kc < col_start + NA_WIN_COLS)
    sel = (kc - qc + NA_WIN_COLS - 1)[None] == np.arange(2 * NA_WIN_COLS - 1)[:, None, None]
    by_row = jnp.einsum("hrd,dqk->hrqk", rpb.astype(F32), jnp.asarray(sel & col_ok[None], F32),
                        precision=lax.Precision.HIGHEST)
    by_row = jnp.where(col_ok[None, None], by_row, NEG_BIG)
    masked = jnp.full((h, GRID_W, GRID_W), NEG_BIG, F32)
    tabs = []
    for off, first_row in ((0, lambda qr: 0), (NA_WIN_ROWS // 2, lambda qr: qr), (NA_WIN_ROWS, lambda qr: NA_ROWS)):
        q_rows = []
        for qr in range(NA_ROWS):
            lo = first_row(qr)
            blocks = [by_row[:, kr - off - qr + NA_WIN_ROWS - 1] if lo <= kr < lo + NA_WIN_ROWS else masked
                      for kr in range(NA_KROWS)]
            q_rows.append(jnp.concatenate(blocks, axis=2))
        tabs.append(jnp.concatenate(q_rows, axis=1))
    return jnp.stack(tabs)


def _rw_prep_body(n_ctx, n_tok, x_ref, xp_ref, xn_ref, mod_ref, g1_ref, wrz_ref, mu_ref, w0_ref, w2_ref, a0_ref, a2_ref,
                  kk_ref, ka_ref, seg_ref, r_ref, v_ref, kkn_ref, lw_ref, beta_ref, kd_ref, gd_ref):
    tm = x_ref.shape[0]
    row0 = pl.program_id(0) * tm
    x_ext = jnp.concatenate([xp_ref[0], x_ref[...], xn_ref[0]], axis=0)
    h = _modulated_norm(x_ext, g1_ref[...], mod_ref[...], row0 - SUBLANE, n_ctx, 1).astype(BF16)
    z_ext = _dot(h, wrz_ref[...])
    z = z_ext[SUBLANE:SUBLANE + tm]
    loc = lax.broadcasted_iota(jnp.int32, (tm, 1), 0)
    rows = row0 + loc
    zp = jnp.where(loc == 0, z_ext[SUBLANE - 1:SUBLANE, :], pltpu.roll(z, 1, 0))
    zp = jnp.where((rows == 0) | (rows == n_ctx), 0.0, zp)
    zn = jnp.where(loc == tm - 1, z_ext[SUBLANE + tm:SUBLANE + tm + 1, :], pltpu.roll(z, tm - 1, 0))
    zn = jnp.where((rows == n_ctx - 1) | (rows == n_tok - 1), 0.0, zn)
    mu = mu_ref[...]
    zs = z + mu[0:1] * (zp - z) + mu[1:2] * (zn - z)
    w = RW_WIDTH
    r, k, v = zs[:, :w], zs[:, w:2 * w], zs[:, 2 * w:3 * w]
    wd = zs[:, 3 * w:3 * w + 2 * RW_DECAY_LORA]
    ad = zs[:, 3 * w + 2 * RW_DECAY_LORA:3 * w + 2 * RW_DECAY_LORA + 2 * RW_ICLR_LORA]
    gd_ref[...] = zs[:, RW_IN - RW_GATE_LORA:]
    u = w0_ref[...] + _dot(jnp.tanh(wd).astype(BF16), w2_ref[...])
    a = jax.nn.sigmoid(a0_ref[...] + _dot(ad.astype(BF16), a2_ref[...]))
    lw = -jax.nn.sigmoid(u) * math.exp(-0.5)
    kkr = k * kk_ref[...]
    kkn = kkr / jnp.maximum(jnp.sqrt(_seg_sum(kkr * kkr, seg_ref[...])), 1e-12)
    r_ref[...] = r
    v_ref[...] = v
    kkn_ref[...] = kkn
    for d in range(2):
        a_d = a[:, d * w:(d + 1) * w]
        lw_ref[d] = lw[:, d * w:(d + 1) * w]
        beta_ref[d] = kkn * a_d
        kd_ref[d] = k * (1.0 + (a_d - 1.0) * ka_ref[...])


def _rw_prep(x_all, mod_l, g1, w_rz, mu, w0, w2b, a0, a2b, k_k, k_a, seg, n_ctx):
    t, d = x_all.shape
    n8 = TM // SUBLANE
    x3 = x_all.reshape(t // SUBLANE, SUBLANE, d)
    tok = lambda w: pl.BlockSpec((TM, w), lambda i: (i, 0))
    tok2 = pl.BlockSpec((2, TM, RW_WIDTH), lambda i: (0, i, 0))
    f = jax.ShapeDtypeStruct((t, RW_WIDTH), F32)
    f2 = jax.ShapeDtypeStruct((2, t, RW_WIDTH), F32)
    smalls = (mod_l, g1, w_rz, mu, w0, w2b, a0, a2b, k_k, k_a, seg)
    return pl.pallas_call(
        functools.partial(_rw_prep_body, n_ctx, t),
        grid=(t // TM,),
        in_specs=[tok(d),
                  pl.BlockSpec((1, SUBLANE, d), lambda i: (jnp.maximum(i * n8 - 1, 0), 0, 0)),
                  pl.BlockSpec((1, SUBLANE, d), lambda i: (jnp.minimum((i + 1) * n8, t // SUBLANE - 1), 0, 0))]
                 + [_full(a.shape) for a in smalls],
        out_specs=[tok(RW_WIDTH), tok(RW_WIDTH), tok(RW_WIDTH), tok2, tok2, tok2, tok(RW_GATE_LORA)],
        out_shape=[f, f, f, f2, f2, f2, jax.ShapeDtypeStruct((t, RW_GATE_LORA), F32)],
        compiler_params=_cp(1),
        name="rw_prep",
    )(x_all, x3, x3, *smalls)


def _rw_group_steps(probs):
    c, gl = probs[0][0].shape
    n = gl // RW_GROUP
    revs = [p[6] for p in probs]
    t_i = lax.broadcasted_iota(jnp.int32, (c, c), 0)
    s_i = lax.broadcasted_iota(jnp.int32, (c, c), 1)
    tri = {False: (s_i <= t_i).astype(BF16), True: (s_i >= t_i).astype(BF16)}
    diag_blk = (lax.broadcasted_iota(jnp.int32, (RW_GROUP * c, gl), 0) // c
                == lax.broadcasted_iota(jnp.int32, (RW_GROUP * c, gl), 1) // n)
    tt = lax.broadcasted_iota(jnp.int32, (c, RW_GROUP * c), 0)
    ss = lax.broadcasted_iota(jnp.int32, (c, RW_GROUP * c), 1) % c
    strict = {False: ss < tt, True: ss > tt}
    incl4 = {False: ss <= tt, True: ss >= tt}
    eye4 = jnp.where(ss == tt, 1.0, 0.0)

    def x4(m):
        return jnp.where(diag_blk, jnp.concatenate([m] * RW_GROUP, axis=0), jnp.zeros((), m.dtype))

    def each(fn, *lists):
        return [fn(*args) for args in zip(*lists)]

    lws = [p[3] for p in probs]
    parts = [_split3(lw) for lw in lws]
    g_inc = each(lambda rev, hml: _dot(tri[rev], hml[0]) + _dot(tri[rev], hml[1]) + _dot(tri[rev], hml[2]), revs, parts)
    e_neg = [jnp.exp(-g) for g in g_inc]
    a_t = each(lambda p, g: (-p[2] * jnp.exp(g - p[3])).astype(BF16), probs, g_inc)
    b_t = each(lambda p, e: (p[4] * e).astype(BF16), probs, e_neg)
    k_t = each(lambda p, e: (p[5] * e).astype(BF16), probs, e_neg)
    r_t = each(lambda p, g: (p[0] * jnp.exp(g)).astype(BF16), probs, g_inc)
    v_bf = [p[1].astype(BF16) for p in probs]
    decay_c = [jnp.exp(jnp.sum(lw, axis=0, keepdims=True)) for lw in lws]

    xb, xk, xv, xa = [each(x4, m) for m in (b_t, k_t, v_bf, a_t)]
    ar = each(lambda a, r: jnp.concatenate([a, r], axis=0), a_t, r_t)
    g_b = each(_dot_nt, ar, xb)
    g_k = each(_dot_nt, ar, xk)
    l_ab = each(lambda rev, g: jnp.where(strict[rev], g[:c], 0.0), revs, g_b)
    l_ak = each(lambda rev, g: jnp.where(strict[rev], g[:c], 0.0).astype(BF16), revs, g_k)
    m_rb = each(lambda rev, g: jnp.where(incl4[rev], g[c:], 0.0).astype(BF16), revs, g_b)
    m_rk = each(lambda rev, g: jnp.where(incl4[rev], g[c:], 0.0).astype(BF16), revs, g_k)
    lm_v = each(lambda l, m, x: _dot(jnp.concatenate([l, m], axis=0), x), l_ak, m_rk, xv)
    lv = [t[:c].astype(BF16) for t in lm_v]
    y_v = [t[c:] for t in lm_v]

    pw = [l.astype(BF16) for l in l_ab]
    tinv = [eye4 + l for l in l_ab]
    pw = each(lambda p: _dot(p, x4(p)).astype(BF16), pw)
    k2 = 2
    while 2 * k2 < c:
        tp = each(lambda t, p: _dot(jnp.concatenate([t.astype(BF16), p], axis=0), x4(p)), tinv, pw)
        tinv = each(lambda t, r: t + r[:c], tinv, tp)
        pw = [r[c:].astype(BF16) for r in tp]
        k2 *= 2
    tinv = each(lambda t, p: (t + _dot(t.astype(BF16), x4(p))).astype(BF16), tinv, pw)

    wm = each(lambda t, x: _dot(t, x).astype(BF16), tinv, xa)
    u0 = each(lambda t, l: _dot(t, x4(l)), tinv, lv)
    wr = each(lambda w, r: jnp.concatenate([w, r], axis=0), wm, r_t)
    bk = each(lambda b, k: jnp.concatenate([b, k], axis=0), b_t, k_t)
    dcol = [jnp.concatenate([jnp.transpose(jnp.broadcast_to(d, (LANE, gl)))] * (gl // LANE), axis=1) for d in decay_c]

    def with_state(idxs, h_bf):
        pick = lambda lst: [lst[i] for i in idxs]
        wr_h = each(_dot, pick(wr), h_bf)
        u_bf = each(lambda t, u: (t[:c] + u).astype(BF16), wr_h, pick(u0))
        y = each(lambda t, mb, u, yv: t[c:] + _dot(mb, x4(u)) + yv, wr_h, pick(m_rb), u_bf, pick(y_v))
        dh = each(lambda b, u, v: _dot_tn(b, jnp.concatenate([u, v], axis=0)), pick(bk), u_bf, pick(v_bf))
        return list(zip(y, dh, pick(dcol)))

    return with_state


def _rw_chunk_body(rf_ref, vf_ref, kf_ref, lwf_ref, bf_ref, kdf_ref,
                   rb_ref, vb_ref, kb_ref, lwb_ref, bb_ref, kdb_ref, yf_ref, yb_ref, h_ref):
    @pl.when(pl.program_id(0) == 0)
    def _():
        h_ref[...] = jnp.zeros_like(h_ref)

    c = RW_CHUNK
    gl = RW_GROUP * RW_HEAD_DIM
    n_grp = RW_WIDTH // gl
    blk = (lax.broadcasted_iota(jnp.int32, (gl, gl), 0) // RW_HEAD_DIM
           == lax.broadcasted_iota(jnp.int32, (gl, gl), 1) // RW_HEAD_DIM)
    dirs = ((rf_ref, vf_ref, kf_ref, lwf_ref, bf_ref, kdf_ref, yf_ref, False),
            (rb_ref, vb_ref, kb_ref, lwb_ref, bb_ref, kdb_ref, yb_ref, True))
    probs, waves = [], [[] for _ in range(RW_CHUNKS_PER_STEP)]
    for d, (r_ref, v_ref, k_ref, lw_ref, b_ref, kd_ref, y_ref, rev) in enumerate(dirs):
        for w in range(RW_CHUNKS_PER_STEP):
            ci = RW_CHUNKS_PER_STEP - 1 - w if rev else w
            rows = slice(ci * c, (ci + 1) * c)
            for g in range(n_grp):
                sl = slice(g * gl, (g + 1) * gl)
                waves[w].append((len(probs), y_ref, rows, sl, d * n_grp + g))
                probs.append((r_ref[rows, sl], v_ref[rows, sl], k_ref[rows, sl], lw_ref[0][rows, sl],
                              b_ref[0][rows, sl], kd_ref[0][rows, sl], rev))
    with_state = _rw_group_steps(probs)
    for wave in waves:
        res = with_state([p for p, *_ in wave], [h_ref[idx].astype(BF16) for *_, idx in wave])
        for (_, y_ref, rows, sl, idx), (y, dh, dec) in zip(wave, res):
            y_ref[rows, sl] = y
            h_ref[idx] = (h_ref[idx] + jnp.where(blk, dh, 0.0)) * dec


def _rw_chunks(r, v, kkn, lw, beta, kd, n_ctx):
    t = r.shape[0]
    c = RW_CHUNK * RW_CHUNKS_PER_STEP
    nc = t // c
    nc_ctx = n_ctx // c

    def fwd(i):
        return i

    def bwd(i):
        return jnp.where(i < nc_ctx, nc_ctx - 1 - i, nc - 1 - (i - nc_ctx))

    def specs(m):
        one = pl.BlockSpec((c, RW_WIDTH), lambda i: (m(i), 0))
        return one

    def specs2(m, d):
        return pl.BlockSpec((1, c, RW_WIDTH), lambda i: (d, m(i), 0))

    in_specs = [specs(fwd), specs(fwd), specs(fwd), specs2(fwd, 0), specs2(fwd, 0), specs2(fwd, 0),
                specs(bwd), specs(bwd), specs(bwd), specs2(bwd, 1), specs2(bwd, 1), specs2(bwd, 1)]
    f = jax.ShapeDtypeStruct((t, RW_WIDTH), F32)
    n_state = 2 * RW_HEADS // RW_GROUP
    gl = RW_GROUP * RW_HEAD_DIM
    return pl.pallas_call(
        _rw_chunk_body,
        grid=(nc,),
        in_specs=in_specs,
        out_specs=[specs(fwd), specs(bwd)],
        out_shape=[f, f],
        scratch_shapes=[pltpu.VMEM((n_state, gl, gl), F32)],
        compiler_params=_cp(1),
        name="rw_chunks",
    )(r, v, kkn, lw, beta, kd, r, v, kkn, lw, beta, kd)


def _merge_body(n_ctx, t_off, x_ref, mod_ref, g1_ref, g2n_ref, wg_ref, yat_ref, yb_ref, yf_ref, ybw_ref, r_ref, v_ref,
                kd_ref, gd_ref, g2_ref, rk_ref, lng_ref, lnb_ref, seg_ref, woa_ref, wob_ref, wor_ref, wout_ref,
                xmid_ref, h2_ref):
    tm, d = x_ref.shape
    row0 = (pl.program_id(0) + t_off) * tm
    x = x_ref[...]
    mod = mod_ref[...]
    h = _modulated_norm(x, g1_ref[...], mod, row0, n_ctx, 1).astype(BF16)
    gates = jax.nn.sigmoid(_dot(h, wg_ref[...]))
    pa = _dot_tn(yat_ref[...].astype(BF16), woa_ref[...])
    pb = _dot(yb_ref[...].astype(BF16), wob_ref[...])
    seg = seg_ref[...]
    inv = 1.0 / RW_HEAD_DIM
    y = yf_ref[...] + ybw_ref[...]
    dy = y - _seg_sum(y, seg) * inv
    yn = dy * lax.rsqrt(_seg_sum(dy * dy, seg) * inv + RW_GN_EPS) * lng_ref[...] + lnb_ref[...]
    bonus = _seg_sum(r_ref[...] * (kd_ref[0] + kd_ref[1]) * rk_ref[...], seg) * v_ref[...]
    g = _dot(jax.nn.sigmoid(gd_ref[...]).astype(BF16), g2_ref[...])
    pr = _dot(((yn + bonus) * g).astype(BF16), wor_ref[...])
    m = gates[:, :d] * pa + gates[:, d:2 * d] * pb + gates[:, 2 * d:] * pr
    o = _dot(m.astype(BF16), wout_ref[...])
    x_mid = x + _gate_rows(mod, row0, tm, n_ctx, 1) * o
    xmid_ref[...] = x_mid
    h2_ref[...] = _modulated_norm(x_mid, g2n_ref[...], mod, row0, n_ctx, 2).astype(h2_ref.dtype)


def _merge(x_all, mod_l, g1, g2n, wg, yat, yb, yf, ybw, r, v, kd, gd, g2, rk, lng, lnb, seg, woa, wob, wor, wout,
           n_ctx, with_ctx, h2_dtype):
    t, d = x_all.shape
    t_off = 0 if with_ctx else n_ctx // TM
    n_rows = yat.shape[1]
    tok = lambda w: pl.BlockSpec((TM, w), lambda i: (i + t_off, 0))
    own = lambda w: pl.BlockSpec((TM, w), lambda i: (i, 0))
    smalls = (g2, rk, lng, lnb, seg, woa, wob, wor, wout)
    return pl.pallas_call(
        functools.partial(_merge_body, n_ctx, t_off),
        grid=(n_rows // TM,),
        in_specs=[tok(d), _full(mod_l.shape), _full(g1.shape), _full(g2n.shape), _full(wg.shape),
                  pl.BlockSpec((yat.shape[0], TM), lambda i: (0, i)), own(NA_WIDTH),
                  tok(RW_WIDTH), tok(RW_WIDTH), tok(RW_WIDTH), tok(RW_WIDTH),
                  pl.BlockSpec((2, TM, RW_WIDTH), lambda i: (0, i + t_off, 0)), tok(RW_GATE_LORA)]
                 + [_full(a.shape) for a in smalls],
        out_specs=[own(d), own(d)],
        out_shape=[jax.ShapeDtypeStruct((n_rows, d), F32), jax.ShapeDtypeStruct((n_rows, d), h2_dtype)],
        compiler_params=_cp(1),
        name="merge",
    )(x_all, mod_l, g1, g2n, wg, yat, yb, yf, ybw, r, v, kd, gd, *smalls)


def _ffn_body(n_ctx, h_ref, x_ref, mod_ref, w1_ref, w3_ref, w2_ref, o_ref):
    tm = h_ref.shape[0]
    h = h_ref[...]
    acc = jnp.zeros(o_ref.shape, F32)
    for f in range(w1_ref.shape[0]):
        a = _dot(h, w1_ref[f])
        b = _dot(h, w3_ref[f])
        acc = acc + _dot((a * jax.nn.sigmoid(a) * b).astype(BF16), w2_ref[f])
    o_ref[...] = x_ref[...] + _gate_rows(mod_ref[...], pl.program_id(0) * tm, tm, n_ctx, 2) * acc


def _ffn(h2, x_mid, mod_l, w1, w3, w2, n_ctx):
    t, d = x_mid.shape
    tok = pl.BlockSpec((TM, d), lambda i: (i, 0))
    return pl.pallas_call(
        functools.partial(_ffn_body, n_ctx),
        grid=(t // TM,),
        in_specs=[tok, tok, _full(mod_l.shape), _full(w1.shape), _full(w3.shape), _full(w2.shape)],
        out_specs=tok,
        out_shape=jax.ShapeDtypeStruct((t, d), F32),
        compiler_params=_cp(1),
        name="ffn",
    )(h2, x_mid, mod_l, w1, w3, w2)


def _router_body(h_ref, w_ref, idx_ref, gate_ref):
    logits = _dot(h_ref[...].astype(BF16), w_ref[...])
    lane_i = lax.broadcasted_iota(jnp.int32, logits.shape, 1)
    lane = lane_i.astype(F32)
    logits = jnp.where(lane_i < N_EXPERTS, logits, NEG_BIG)
    m1 = jnp.max(logits, axis=-1, keepdims=True)
    i1 = jnp.min(jnp.where(logits == m1, lane, float(LANE)), axis=-1, keepdims=True)
    rest = jnp.where(lane == i1, NEG_BIG, logits)
    m2 = jnp.max(rest, axis=-1, keepdims=True)
    i2 = jnp.min(jnp.where(rest == m2, lane, float(LANE)), axis=-1, keepdims=True)
    e2 = jnp.exp(m2 - m1)
    g1 = 1.0 / (1.0 + e2)
    lane = lane_i
    idx_ref[...] = jnp.where(lane == 0, i1, jnp.where(lane == 1, i2, 0.0)).astype(jnp.int32)
    gate_ref[...] = jnp.where(lane == 0, g1, jnp.where(lane == 1, e2 * g1, 0.0))


def _router(h2, w_router_pad):
    n, d = h2.shape
    tok = lambda w: pl.BlockSpec((TM, w), lambda i: (i, 0))
    return pl.pallas_call(
        _router_body,
        grid=(n // TM,),
        in_specs=[tok(d), _full(w_router_pad.shape)],
        out_specs=[tok(LANE), tok(LANE)],
        out_shape=[jax.ShapeDtypeStruct((n, LANE), jnp.int32), jax.ShapeDtypeStruct((n, LANE), F32)],
        compiler_params=_cp(1),
        name="router",
    )(h2, w_router_pad)


def _row_copy(src_hbm, dst_vmem, src_row, dst_row, sem):
    return pltpu.make_async_copy(src_hbm.at[pl.ds(src_row, 1)], dst_vmem.at[pl.ds(dst_row, 1)], sem)


def _moe_ffn_body(be_ref, nb_ref, tok_hbm, h_hbm, w1_ref, w3_ref, w2_ref, o_ref, x_buf, acc_ref, idx_smem,
                  sem_idx, sem_x):
    b, f = pl.program_id(0), pl.program_id(1)
    n_used = nb_ref[0]
    bm = x_buf.shape[1]
    cur, nxt = b % 2, (b + 1) % 2

    def idx_copy(blk, slot):
        return pltpu.make_async_copy(tok_hbm.at[blk], idx_smem.at[slot], sem_idx)

    def start_rows(slot):
        for r in range(bm):
            _row_copy(h_hbm, x_buf.at[slot], idx_smem[slot, r], r, sem_x.at[slot]).start(priority=r % 2)

    def wait_rows(slot):
        pltpu.make_async_copy(h_hbm.at[pl.ds(0, bm)], x_buf.at[slot], sem_x.at[slot]).wait()

    def half(first):
        x = x_buf[cur].astype(BF16)
        a = _dot(x, w1_ref[0])
        g = _dot(x, w3_ref[0])
        y = _dot((a * jax.nn.sigmoid(a) * g).astype(BF16), w2_ref[0])
        if first:
            acc_ref[...] = y
        else:
            o_ref[...] = acc_ref[...] + y

    @pl.when((f == 0) & (b == 0))
    def _():
        cp = idx_copy(0, 0)
        cp.start()
        cp.wait()
        start_rows(0)

    @pl.when((f == 0) & (b < n_used))
    def _():
        wait_rows(cur)

        @pl.when(b + 1 < n_used)
        def _():
            idx_copy(b + 1, nxt).start()

        half(True)

    @pl.when((f == 1) & (b + 1 < n_used))
    def _():
        idx_copy(b + 1, nxt).wait()
        start_rows(nxt)
        half(False)

    @pl.when((f == 1) & (b + 1 == n_used))
    def _():
        half(False)

    @pl.when((f == 1) & (b >= n_used))
    def _():
        o_ref[...] = jnp.zeros_like(o_ref)


def _moe_ffn(blk_expert, n_used, tok_of_slot, h2, w1, w3, w2, fc):
    nb = tok_of_slot.shape[0]
    d = h2.shape[1]
    assert w1.shape[2] == 2 * fc
    grid_spec = pltpu.PrefetchScalarGridSpec(
        num_scalar_prefetch=2,
        grid=(nb, 2),
        in_specs=[pl.BlockSpec(memory_space=pl.ANY), pl.BlockSpec(memory_space=pl.ANY),
                  pl.BlockSpec((1, d, fc), lambda b, f, be, nu: (be[b], 0, f)),
                  pl.BlockSpec((1, d, fc), lambda b, f, be, nu: (be[b], 0, f)),
                  pl.BlockSpec((1, fc, d), lambda b, f, be, nu: (be[b], f, 0))],
        out_specs=pl.BlockSpec((MOE_BM, d), lambda b, f, be, nu: (b, 0)),
        scratch_shapes=[pltpu.VMEM((2, MOE_BM, d), F32), pltpu.VMEM((MOE_BM, d), F32),
                        pltpu.SMEM((2, MOE_BM), jnp.int32), pltpu.SemaphoreType.DMA(()),
                        pltpu.SemaphoreType.DMA((2,))],
    )
    return pl.pallas_call(
        _moe_ffn_body,
        grid_spec=grid_spec,
        out_shape=jax.ShapeDtypeStruct((nb * MOE_BM, d), F32),
        compiler_params=_cp(2),
        name="moe_ffn",
    )(blk_expert, n_used, tok_of_slot, h2, w1, w3, w2)


def _combine_body(slot_hbm, yg_hbm, gate_ref, x_ref, mod_ref, o_ref, y_buf, idx_smem, sem_idx, sem_rows):
    i = pl.program_id(0)
    nt = pl.num_programs(0)
    tm, d = x_ref.shape
    copies = [pltpu.make_async_copy(slot_hbm.at[c * nt + i], idx_smem.at[c], sem_idx.at[c]) for c in range(TOP_K)]
    for cp in copies:
        cp.start()
    for cp in copies:
        cp.wait()

    def start(g, carry):
        for u in range(GATHER_UNROLL):
            r = g * GATHER_UNROLL + u
            for c in range(TOP_K):
                _row_copy(yg_hbm, y_buf.at[c], idx_smem[c, r], r, sem_rows).start(priority=c)
        return carry

    lax.fori_loop(0, tm // GATHER_UNROLL, start, 0)
    for c in range(TOP_K):
        pltpu.make_async_copy(yg_hbm.at[pl.ds(0, tm)], y_buf.at[c], sem_rows).wait()
    g = gate_ref[...]
    f = g[:, 0:1] * y_buf[0] + g[:, 1:2] * y_buf[1]
    o_ref[...] = x_ref[...] + mod_ref[0:1, 5 * d:6 * d] * f


def _moe_combine(slot, yg, gate_pad, x_mid, mod_l):
    n, d = x_mid.shape
    nt = n // TM
    return pl.pallas_call(
        _combine_body,
        grid=(nt,),
        in_specs=[pl.BlockSpec(memory_space=pl.ANY), pl.BlockSpec(memory_space=pl.ANY),
                  pl.BlockSpec((TM, LANE), lambda i: (i, 0)), pl.BlockSpec((TM, d), lambda i: (i, 0)),
                  _full(mod_l.shape)],
        out_specs=pl.BlockSpec((TM, d), lambda i: (i, 0)),
        out_shape=jax.ShapeDtypeStruct((n, d), F32),
        scratch_shapes=[pltpu.VMEM((TOP_K, TM, d), F32), pltpu.SMEM((TOP_K, TM), jnp.int32),
                        pltpu.SemaphoreType.DMA((TOP_K,)), pltpu.SemaphoreType.DMA(())],
        compiler_params=_cp(1),
        name="moe_combine",
    )(slot.reshape(TOP_K * nt, TM), yg, gate_pad, x_mid, mod_l)


def _moe(h2, x_mid, mod_l, router_w, w1, w3, w2, fc):
    n, d = h2.shape
    e_n = router_w.shape[1]
    idx_pad, gate_pad = _router(h2, jnp.zeros((d, LANE), BF16).at[:, :e_n].set(router_w.astype(BF16)))
    flat_e = idx_pad[:, :TOP_K].T.reshape(-1)
    onehot = (flat_e[:, None] == jnp.arange(e_n)[None, :]).astype(jnp.int32)
    rank = jnp.take_along_axis(jnp.cumsum(onehot, axis=0) - onehot, flat_e[:, None], axis=1)[:, 0]
    counts = jnp.sum(onehot, axis=0)
    padded = (counts + MOE_BM - 1) // MOE_BM * MOE_BM
    pad_ends = jnp.cumsum(padded)
    slot = ((pad_ends - padded)[flat_e] + rank).astype(jnp.int32)
    nb = TOP_K * n // MOE_BM + e_n
    n_slots = nb * MOE_BM
    tok_of_slot = jnp.zeros((n_slots,), jnp.int32).at[slot].set(jnp.tile(jnp.arange(n, dtype=jnp.int32), TOP_K))
    blk_expert = jnp.minimum(jnp.searchsorted(pad_ends, jnp.arange(nb, dtype=jnp.int32) * MOE_BM, side="right"),
                             e_n - 1).astype(jnp.int32)
    n_used = (pad_ends[-1:] // MOE_BM).astype(jnp.int32)
    yg = _moe_ffn(blk_expert, n_used, tok_of_slot.reshape(nb, MOE_BM), h2, w1, w3, w2, fc)
    return _moe_combine(slot, yg, gate_pad, x_mid, mod_l)


def _head_lanes(nope=None, rope=None):
    half = MLA_ROPE // 4
    lead = (nope if nope is not None else rope).shape[:-1]
    dt = (nope if nope is not None else rope).dtype
    z = lambda n: jnp.zeros(lead + (n,), dt)
    n_lo = 64 - 2 * half
    nope_lo, nope_hi = (nope[..., :n_lo], nope[..., n_lo:]) if nope is not None else (z(n_lo), z(MLA_NOPE - n_lo))
    r1, r2, c1, c2 = ([rope[..., i * half:(i + 1) * half] for i in range(4)] if rope is not None else [z(half)] * 4)
    return jnp.concatenate([r1, c1, nope_lo, r2, c2, nope_hi, z(HEAD_PAD - MLA_QK)], axis=-1)


def _rope_rows():
    half = MLA_ROPE // 4
    freqs = np.exp(-math.log(ROPE_BASE) * np.arange(half, dtype=np.float32) / half).astype(np.float32)
    tab = np.zeros((SUBLANE, HEAD_PAD), np.float32)
    for base, sign in ((0, -1.0), (64, 1.0)):
        tab[0, base:base + half] = freqs
        tab[1, base + half:base + 2 * half] = freqs
        tab[2, base:base + 2 * half] = sign
    return jnp.asarray(tab)


def _seg_matrix(width, seg):
    i = np.arange(width)
    return jnp.asarray((i[:, None] // seg == i[None, :] // seg).astype(np.float32), dtype=BF16)


def _block_diag2(w):
    _, r, c = w.shape
    z = jnp.zeros((r, c), w.dtype)
    return jnp.concatenate([jnp.concatenate([w[0], z], axis=1), jnp.concatenate([z, w[1]], axis=1)], axis=0)


def kernel(x, c, ctx, c_ctx, mod_w, mod_b, norm1_g, norm2_g, w_in, mla_cq_g, mla_wuq, mla_ckv_g, mla_wukv, mla_qn_g, mla_kn_g, mla_wo, na_qn_g, na_kn_g, na_rpb, na_wo, rw_mu, rw_w0, rw_w2, rw_a0, rw_a2, rw_g2, rw_kk, rw_ka, rw_rk, rw_ln_g, rw_ln_b, rw_wo, w_out, ffn_w1, ffn_w3, ffn_w2, moe_router, moe_w1, moe_w3, moe_w2):
    b, n_lat, d = x.shape
    n_ctx = ctx.shape[1]
    depth = mod_w.shape[0]
    assert b == 1 and n_ctx % TM == 0 and n_lat % TM == 0 and n_lat % (NA_ROWS * GRID_W) == 0
    assert n_lat // GRID_W >= NA_KROWS and n_ctx % (RW_CHUNK * RW_CHUNKS_PER_STEP) == 0

    mod = _modulation(c, c_ctx, mod_w, mod_b)
    x_all = jnp.concatenate([ctx[0], x[0]], axis=0)
    rope_rows = _rope_rows()
    seg64 = _seg_matrix(RW_WIDTH, RW_HEAD_DIM)

    o_ckv = MLA_Q_LORA
    o_kr = o_ckv + MLA_KV_LORA
    o_na = o_kr + MLA_ROPE
    o_rz = o_na + 3 * NA_WIDTH
    o_gate = o_rz + RW_IN

    for l in range(depth):
        need_ctx = l < depth - 1
        mod_l = mod[l]
        g1 = norm1_g[l][None, :]
        g2n = norm2_g[l][None, :]
        wi = w_in[l]

        w_a = jnp.concatenate([wi[:, :o_kr], _head_lanes(rope=wi[:, o_kr:o_na])], axis=1).astype(BF16)
        w_na = wi[:, o_na:o_rz].astype(BF16)
        w_rz = wi[:, o_rz:o_gate].astype(BF16)
        w_g = wi[:, o_gate:].astype(BF16)
        wuq3 = mla_wuq[l].reshape(MLA_Q_LORA, MLA_HEADS, MLA_QK)
        wuq = _head_lanes(wuq3[..., :MLA_NOPE], wuq3[..., MLA_NOPE:]).reshape(MLA_Q_LORA, -1).astype(BF16)
        wukv = mla_wukv[l].reshape(MLA_KV_LORA, MLA_HEADS, MLA_NOPE + MLA_V)
        wuk = _head_lanes(nope=wukv[:, :, :MLA_NOPE]).reshape(MLA_KV_LORA, -1).astype(BF16)
        wuvt = wukv[:, :, MLA_NOPE:].reshape(MLA_KV_LORA, -1).T.astype(BF16)
        qng = _head_lanes(mla_qn_g[l][:MLA_NOPE], mla_qn_g[l][MLA_NOPE:])[None, :]
        kng = _head_lanes(mla_kn_g[l][:MLA_NOPE], mla_kn_g[l][MLA_NOPE:])[None, :]
        bound = 1.01 * math.sqrt(MLA_QK) * jnp.max(jnp.abs(mla_qn_g[l])) * jnp.max(jnp.abs(mla_kn_g[l]))
        shift = jnp.zeros((1, HEAD_PAD), F32).at[0, MLA_QK].set(-bound * math.log2(math.e))
        mla_wts = (w_a, mla_cq_g[l][None, :], wuq, mla_ckv_g[l][None, :], wuk, wuvt, qng, kng, shift, rope_rows)

        qa, ka, vta = _mla_proj(x_all, mod_l, g1, mla_wts, n_ctx)
        nt_ctx, nt_all = n_ctx // TM, (n_ctx + n_lat) // TM

        def mla_attention(fast):
            def run(qa, ka, vta):
                y = _mla_attn(qa, ka, vta, nt_ctx, nt_all - nt_ctx, nt_all, fast)
                if need_ctx:
                    y = jnp.concatenate([_mla_attn(qa, ka, vta, 0, nt_ctx, nt_ctx, fast), y], axis=1)
                return y
            return run

        yat = lax.cond(bound <= MLA_FAST_BOUND, mla_attention(True), mla_attention(False), qa, ka, vta)

        qb, kb, vb = _na_proj(x_all, mod_l, g1, w_na, jnp.tile(na_qn_g[l], NA_HEADS)[None, :],
                              jnp.tile(na_kn_g[l], NA_HEADS)[None, :], seg64, n_ctx)
        yb = _na_attn(qb, kb, vb, _na_bias_tables(na_rpb[l]), n_ctx)
        if need_ctx:
            yb = jnp.concatenate([_ctx_attn(qb, kb, vb, n_ctx), yb], axis=0)

        r, v, kkn, lw, beta, kd, gd = _rw_prep(
            x_all, mod_l, g1, w_rz, rw_mu[l], rw_w0[l].reshape(1, -1), _block_diag2(rw_w2[l]).astype(BF16), rw_a0[l].reshape(1, -1),
            _block_diag2(rw_a2[l]).astype(BF16), rw_kk[l][None, :], rw_ka[l][None, :], seg64, n_ctx)
        yf, ybw = _rw_chunks(r, v, kkn, lw, beta, kd, n_ctx)

        x_mid, h2 = _merge(x_all, mod_l, g1, g2n, w_g, yat, yb, yf, ybw, r, v, kd, gd,
                           rw_g2[l].astype(BF16), rw_rk[l].reshape(1, -1), rw_ln_g[l][None, :], rw_ln_b[l][None, :], seg64,
                           mla_wo[l].astype(BF16), na_wo[l].astype(BF16), rw_wo[l].astype(BF16), w_out[l].astype(BF16),
                           n_ctx, need_ctx, BF16 if l % 2 == 0 else F32)

        if l % 2 == 0:
            fc = FFN_FC
            w1 = ffn_w1[l // 2]
            nf = w1.shape[1] // fc
            w1r = w1.reshape(d, nf, fc).transpose(1, 0, 2).astype(BF16)
            w3r = ffn_w3[l // 2].reshape(d, nf, fc).transpose(1, 0, 2).astype(BF16)
            w2r = ffn_w2[l // 2].reshape(nf, fc, d).astype(BF16)
            if need_ctx:
                x_all = _ffn(h2, x_mid, mod_l, w1r, w3r, w2r, n_ctx)
            else:
                x_lat = _ffn(h2, x_mid, mod_l, w1r, w3r, w2r, 0)
        else:
            moe_args = (moe_router[l // 2], moe_w1[l // 2].astype(BF16), moe_w3[l // 2].astype(BF16),
                        moe_w2[l // 2].astype(BF16), MOE_FC)
            if need_ctx:
                lat = _moe(h2[n_ctx:], x_mid[n_ctx:], mod_l, *moe_args)
                ctx_rows = _moe(h2[:n_ctx], x_mid[:n_ctx], mod_l.at[0].set(mod_l[1]), *moe_args)
                x_all = jnp.concatenate([ctx_rows, lat], axis=0)
            else:
                x_lat = _moe(h2, x_mid, mod_l, *moe_args)
    return x_lat[None]
```
